```python
import math
import jax
import jax.numpy as jnp
from jax import lax
import numpy as np

D_MODEL = 1024
BATCH = 4
SEQ = 8192
DEPTH = 2

N_MIXERS = 2
M_EXPAND = 2
M_D_INNER = M_EXPAND * D_MODEL
M_HEAD_DIM = 64
M_HEADS = M_D_INNER // M_HEAD_DIM
M_GROUPS = 8
M_HPG = M_HEADS // M_GROUPS
M_STATE = 128
M_CONV = 4
M_CHUNK = 128
M_CONV_DIM = M_D_INNER + 2 * M_GROUPS * M_STATE
M_IN_DIM = M_D_INNER + M_CONV_DIM + M_HEADS
M_NORM_GROUP = M_D_INNER // M_GROUPS
A_HEADS = D_MODEL // 128
A_HEAD_DIM = D_MODEL // A_HEADS // 2
A_V_DIM = 2 * A_HEAD_DIM
A_QK_DIM = A_HEADS * 2 * A_HEAD_DIM
A_QBLOCK = 128
FFN_DIM = ((8 * D_MODEL // 3 + 255) // 256) * 256
N_EXPERTS = 8
TOP_K = 2
EXPERT_DIM = 7 * D_MODEL // 2
MOE_BLOCK = 256
EPS = 1e-6

kernel_name = "hybrid_ssd_diffattn_moe_adaln"


def rms_norm(x):
    xf = x.astype(jnp.float32)
    y = xf * lax.rsqrt(jnp.mean(xf * xf, axis=-1, keepdims=True) + EPS)
    return y.astype(x.dtype)


def causal_segsum(a_cs):
    t = a_cs.shape[-1]
    mask = jnp.tril(jnp.ones((t, t), dtype=bool))
    diff = a_cs[..., :, None] - a_cs[..., None, :]
    return jnp.where(mask, diff, -jnp.inf)


def ssd_chunked(xdt, dta, bm, cm):
    b, l, g, r, p = xdt.shape
    n = bm.shape[-1]
    nc = l // M_CHUNK
    xc = xdt.reshape(b, nc, M_CHUNK, g, r, p)
    bc = bm.reshape(b, nc, M_CHUNK, g, n)
    cc = cm.reshape(b, nc, M_CHUNK, g, n)
    ac = dta.reshape(b, nc, M_CHUNK, g, r).transpose(0, 1, 3, 4, 2)
    acs = jnp.cumsum(ac, axis=-1)
    decay_in = jnp.exp(causal_segsum(acs))
    cb = jnp.einsum('bcqgn,bcsgn->bcgqs', cc, bc)
    y_diag = jnp.einsum('bcgrqs,bcsgrp->bcqgrp', cb[:, :, :, None] * decay_in, xc)
    decay_states = jnp.exp(acs[..., -1:] - acs)
    states = jnp.einsum('bcqgn,bcgrq,bcqgrp->bcgrpn', bc, decay_states, xc)
    chunk_a = jnp.pad(acs[..., -1].transpose(0, 2, 3, 1), ((0, 0), (0, 0), (0, 0), (1, 0)))
    decay_chunk = jnp.exp(causal_segsum(jnp.cumsum(chunk_a, axis=-1)))
    states = jnp.concatenate([jnp.zeros_like(states[:, :1]), states], axis=1)
    new_states = jnp.einsum('bgrzc,bcgrpn->bzgrpn', decay_chunk, states)
    prev_states = new_states[:, :-1]
    y_off = jnp.einsum('bcqgn,bcgrpn,bcgrq->bcqgrp', cc, prev_states, jnp.exp(acs))
    return (y_diag + y_off).reshape(b, l, g, r, p).astype(xdt.dtype)


def mamba2_mixer(h, w_in, conv_w, conv_b, dt_bias, a_log, d_skip, norm_g, w_out):
    b, s, _ = h.shape
    zxbcdt = h @ w_in
    z, xbc, dt_raw = jnp.split(zxbcdt, [M_D_INNER, M_D_INNER + M_CONV_DIM], axis=-1)
    xbc = lax.conv_general_dilated(
        xbc, conv_w[:, None, :], window_strides=(1,), padding=[(M_CONV - 1, 0)],
        dimension_numbers=('NWC', 'WIO', 'NWC'), feature_group_count=M_CONV_DIM) + conv_b
    xbc = jax.nn.silu(xbc)
    xs, bm, cm = jnp.split(xbc, [M_D_INNER, M_D_INNER + M_GROUPS * M_STATE], axis=-1)
    xs = xs.reshape(b, s, M_GROUPS, M_HPG, M_HEAD_DIM)
    bm = bm.reshape(b, s, M_GROUPS, M_STATE)
    cm = cm.reshape(b, s, M_GROUPS, M_STATE)
    dt = jax.nn.softplus((dt_raw + dt_bias).astype(jnp.float32)).reshape(b, s, M_GROUPS, M_HPG)
    a = -jnp.exp(a_log.astype(jnp.float32)).reshape(M_GROUPS, M_HPG)
    y = ssd_chunked(xs * dt[..., None].astype(xs.dtype), dt * a, bm, cm)
    y = y + xs * d_skip.reshape(M_GROUPS, M_HPG, 1)
    y = y.reshape(b, s, M_D_INNER) * jax.nn.silu(z)
    y = rms_norm(y.reshape(b, s, M_GROUPS, M_NORM_GROUP)).reshape(b, s, M_D_INNER) * norm_g
    return y @ w_out


def diff_attention(h, w_qkv, lam_q1, lam_k1, lam_q2, lam_k2, subln_g, w_o, lambda_init):
    b, s, _ = h.shape
    qkv = h @ w_qkv
    q, k, v = jnp.split(qkv, [A_QK_DIM, 2 * A_QK_DIM], axis=-1)
    q = q.reshape(b, s, A_HEADS, 2, A_HEAD_DIM)
    k = k.reshape(b, s, A_HEADS, 2, A_HEAD_DIM)
    v = v.reshape(b, s, A_HEADS, A_V_DIM)
    lam = (jnp.exp(jnp.sum(lam_q1.astype(jnp.float32) * lam_k1.astype(jnp.float32)))
           - jnp.exp(jnp.sum(lam_q2.astype(jnp.float32) * lam_k2.astype(jnp.float32)))
           + lambda_init)
    scale = A_HEAD_DIM ** -0.5
    nb = s // A_QBLOCK
    qb = q.reshape(b, nb, A_QBLOCK, A_HEADS, 2, A_HEAD_DIM).transpose(1, 0, 2, 3, 4, 5)
    k_pos = jnp.arange(s, dtype=jnp.int32)

    def query_block(args):
        qi, bi = args
        sc = jnp.einsum('bqhjd,bkhjd->bhjqk', qi, k).astype(jnp.float32) * scale
        q_pos = bi * A_QBLOCK + jnp.arange(A_QBLOCK, dtype=jnp.int32)
        sc = jnp.where(k_pos[None, :] <= q_pos[:, None], sc, -jnp.inf)
        pr = jax.nn.softmax(sc, axis=-1)
        att = pr[:, :, 0] - lam * pr[:, :, 1]
        return jnp.einsum('bhqk,bkhe->bqhe', att.astype(v.dtype), v)

    o = lax.map(query_block, (qb, jnp.arange(nb, dtype=jnp.int32)))
    o = o.transpose(1, 0, 2, 3, 4).reshape(b, s, A_HEADS, A_V_DIM)
    o = rms_norm(o) * subln_g * (1.0 - lambda_init)
    return o.reshape(b, s, A_HEADS * A_V_DIM) @ w_o


def swiglu(h, w_gate, w_up, w_down):
    return (jax.nn.silu(h @ w_gate) * (h @ w_up)) @ w_down


def moe_swiglu(h, w_router, w_gate, w_up, w_down):
    b, s, d = h.shape
    t = b * s
    xt = h.reshape(t, d)
    logits = (xt @ w_router).astype(jnp.float32)
    top_val, top_idx = lax.top_k(logits, TOP_K)
    gates = jax.nn.softmax(top_val, axis=-1)
    e_flat = top_idx.reshape(-1).astype(jnp.int32)
    t_flat = jnp.repeat(jnp.arange(t, dtype=jnp.int32), TOP_K)
    g_flat = gates.reshape(-1)
    order = jnp.argsort(e_flat)
    e_s, t_s, g_s = e_flat[order], t_flat[order], g_flat[order]
    counts = jnp.bincount(e_flat, length=N_EXPERTS).astype(jnp.int32)
    starts = jnp.cumsum(counts) - counts
    padded = (counts + MOE_BLOCK - 1) // MOE_BLOCK * MOE_BLOCK
    pends = jnp.cumsum(padded)
    pstarts = pends - padded
    n_assign = TOP_K * t
    dest = pstarts[e_s] + (jnp.arange(n_assign, dtype=jnp.int32) - starts[e_s])
    n_rows = n_assign + N_EXPERTS * MOE_BLOCK
    n_blocks = n_rows // MOE_BLOCK
    row_tok = jnp.full((n_rows,), t, dtype=jnp.int32).at[dest].set(t_s)
    row_gate = jnp.zeros((n_rows,), jnp.float32).at[dest].set(g_s)
    block_start = jnp.arange(n_blocks, dtype=jnp.int32) * MOE_BLOCK
    block_exp = jnp.minimum(jnp.searchsorted(pends, block_start, side='right'), N_EXPERTS - 1)
    x_pad = jnp.concatenate([xt, jnp.zeros((1, d), xt.dtype)], axis=0)

    def expert_block(args):
        tok, e = args
        xb = x_pad[tok]
        return (jax.nn.silu(xb @ w_gate[e]) * (xb @ w_up[e])) @ w_down[e]

    y = lax.map(expert_block, (row_tok.reshape(n_blocks, MOE_BLOCK), block_exp))
    y = y.reshape(n_rows, d) * row_gate[:, None].astype(y.dtype)
    out = jnp.zeros((t + 1, d), y.dtype).at[row_tok].add(y)[:t]
    return out.reshape(b, s, d)


def adaln_params(c, ada_w, ada_b):
    mod = jax.nn.silu(c) @ ada_w + ada_b
    return [m[:, None, :] for m in jnp.split(mod, 6, axis=-1)]


def _w(key, shape, fan_in):
    return jax.random.normal(key, shape, jnp.float32) * (fan_in ** -0.5)


def setup_inputs(seed: int = 0) -> dict:
    key = jax.random.key(seed)
    ks = jax.random.split(key, 32)
    f32 = jnp.float32
    u = jax.random.uniform(ks[8], (M_HEADS,), f32)
    dt0 = jnp.exp(u * (math.log(0.1) - math.log(0.001)) + math.log(0.001))
    return {
        'x': jax.random.normal(ks[0], (BATCH, SEQ, D_MODEL), f32),
        'c': jax.random.normal(ks[1], (BATCH, D_MODEL), f32),
        'ada_w0': _w(ks[2], (D_MODEL, 6 * D_MODEL), D_MODEL),
        'ada_b0': 0.02 * jax.random.normal(ks[3], (6 * D_MODEL,), f32),
        'm_w_in': _w(ks[4], (D_MODEL, M_IN_DIM), D_MODEL),
        'm_conv_w': _w(ks[5], (M_CONV, M_CONV_DIM), M_CONV),
        'm_conv_b': 0.02 * jax.random.normal(ks[6], (M_CONV_DIM,), f32),
        'm_dt_bias': dt0 + jnp.log(-jnp.expm1(-dt0)),
        'm_a_log': jnp.log(jax.random.uniform(ks[9], (M_HEADS,), f32, 1.0, 16.0)),
        'm_d_skip': 1.0 + 0.1 * jax.random.normal(ks[10], (M_HEADS,), f32),
        'm_norm_g': 1.0 + 0.1 * jax.random.normal(ks[11], (M_D_INNER,), f32),
        'm_w_out': _w(ks[12], (M_D_INNER, D_MODEL), M_D_INNER),
        'ffn_w_gate': _w(ks[13], (D_MODEL, FFN_DIM), D_MODEL),
        'ffn_w_up': _w(ks[14], (D_MODEL, FFN_DIM), D_MODEL),
        'ffn_w_down': _w(ks[15], (FFN_DIM, D_MODEL), FFN_DIM),
        'ada_w1': _w(ks[16], (D_MODEL, 6 * D_MODEL), D_MODEL),
        'ada_b1': 0.02 * jax.random.normal(ks[17], (6 * D_MODEL,), f32),
        'a_w_qkv': _w(ks[18], (D_MODEL, 2 * A_QK_DIM + A_HEADS * A_V_DIM), D_MODEL),
        'a_lam_q1': 0.1 * jax.random.normal(ks[19], (A_HEAD_DIM,), f32),
        'a_lam_k1': 0.1 * jax.random.normal(ks[20], (A_HEAD_DIM,), f32),
        'a_lam_q2': 0.1 * jax.random.normal(ks[21], (A_HEAD_DIM,), f32),
        'a_lam_k2': 0.1 * jax.random.normal(ks[22], (A_HEAD_DIM,), f32),
        'a_subln_g': 1.0 + 0.1 * jax.random.normal(ks[23], (A_V_DIM,), f32),
        'a_w_o': _w(ks[24], (A_HEADS * A_V_DIM, D_MODEL), A_HEADS * A_V_DIM),
        'moe_w_router': _w(ks[25], (D_MODEL, N_EXPERTS), D_MODEL),
        'moe_w_gate': _w(ks[26], (N_EXPERTS, D_MODEL, EXPERT_DIM), D_MODEL),
        'moe_w_up': _w(ks[27], (N_EXPERTS, D_MODEL, EXPERT_DIM), D_MODEL),
        'moe_w_down': _w(ks[28], (N_EXPERTS, EXPERT_DIM, D_MODEL), EXPERT_DIM),
        'final_g': 1.0 + 0.1 * jax.random.normal(ks[29], (D_MODEL,), f32),
    }


def reference(x, c, ada_w0, ada_b0, m_w_in, m_conv_w, m_conv_b, m_dt_bias, m_a_log, m_d_skip,
              m_norm_g, m_w_out, ffn_w_gate, ffn_w_up, ffn_w_down, ada_w1, ada_b1, a_w_qkv,
              a_lam_q1, a_lam_k1, a_lam_q2, a_lam_k2, a_subln_g, a_w_o, moe_w_router,
              moe_w_gate, moe_w_up, moe_w_down, final_g):
    ada_w = (ada_w0, ada_w1)
    ada_b = (ada_b0, ada_b1)
    for i in range(DEPTH):
        sh1, sc1, g1, sh2, sc2, g2 = adaln_params(c, ada_w[i], ada_b[i])
        h = rms_norm(x) * (1.0 + sc1) + sh1
        if i % N_MIXERS == 0:
            mix = mamba2_mixer(h, m_w_in, m_conv_w, m_conv_b, m_dt_bias, m_a_log, m_d_skip,
                               m_norm_g, m_w_out)
        else:
            lambda_init = 0.8 - 0.6 * math.exp(-0.3 * i)
            mix = diff_attention(h, a_w_qkv, a_lam_q1, a_lam_k1, a_lam_q2, a_lam_k2, a_subln_g,
                                 a_w_o, lambda_init)
        x = x + g1 * mix
        h = rms_norm(x) * (1.0 + sc2) + sh2
        if i % 2 == 0:
            ffn = swiglu(h, ffn_w_gate, ffn_w_up, ffn_w_down)
        else:
            ffn = moe_swiglu(h, moe_w_router, moe_w_gate, moe_w_up, moe_w_down)
        x = x + g2 * ffn
    return rms_norm(x) * final_g
```

```python
import functools
import math

import jax
import jax.numpy as jnp
from jax import lax
from jax.experimental import pallas as pl
from jax.experimental.pallas import tpu as pltpu

F32 = jnp.float32
BF16 = jnp.bfloat16
HIGHEST = lax.Precision.HIGHEST

D_MODEL = 1024
EPS = 1e-6
M_D_INNER = 2048
M_HEAD_DIM = 64
M_HEADS = 32
M_GROUPS = 8
M_HPG = 4
M_STATE = 128
M_CONV = 4
M_CHUNK = 128
M_GROUP_W = M_HPG * M_HEAD_DIM
M_BC_W = 2 * M_GROUPS * M_STATE
M_CONV_DIM = M_D_INNER + M_BC_W
M_MAIN_W = M_D_INNER + M_CONV_DIM
A_HEADS = 8
A_HEAD_DIM = 64
A_V_DIM = 128
A_TQ = 256
A_TK = 512
FFN_DIM = 2816
N_EXPERTS = 8
TOP_K = 2
EXPERT_DIM = 3584
MOE_ROWS = 512
LANES = 128
VMEM_LIMIT = 56 * 1024 * 1024


def _silu(v):
    return v * jax.nn.sigmoid(v)


def _norm_mod(x, sc, sh):
    ms = jnp.mean(x * x, axis=-1, keepdims=True)
    return x * lax.rsqrt(ms + EPS) * (1.0 + sc) + sh


def _cparams(sem):
    return pltpu.CompilerParams(dimension_semantics=sem, vmem_limit_bytes=VMEM_LIMIT)


def _resident(shape):
    nd = len(shape)
    return pl.BlockSpec(shape, lambda *_: (0,) * nd, pipeline_mode=pl.Buffered(1))


def _adaln_kernel(c_ref, w_ref, b_ref, o_ref):
    o_ref[...] = jnp.dot(_silu(c_ref[...]), w_ref[...], preferred_element_type=F32,
                         precision=HIGHEST) + b_ref[...]


def _adaln(c8, w, b):
    rows, d = c8.shape
    n = w.shape[1]
    tn = 1024
    return pl.pallas_call(
        _adaln_kernel,
        grid=(n // tn,),
        in_specs=[pl.BlockSpec((rows, d), lambda j: (0, 0)),
                  pl.BlockSpec((d, tn), lambda j: (0, j)),
                  pl.BlockSpec((1, tn), lambda j: (0, j))],
        out_specs=pl.BlockSpec((rows, tn), lambda j: (0, j)),
        out_shape=jax.ShapeDtypeStruct((rows, n), F32),
        compiler_params=_cparams(("arbitrary",)),
        name="adaln",
    )(c8, w, b)


def _norm_proj_kernel(x_ref, sc_ref, sh_ref, w_ref, *rest, n_chunk, has_aux):
    if has_aux:
        waux_ref, o_ref, aux_ref = rest
    else:
        (o_ref,) = rest
    h = _norm_mod(x_ref[0], sc_ref[0], sh_ref[0]).astype(BF16)
    n = w_ref.shape[1]
    for c0 in range(0, n, n_chunk):
        o_ref[0, :, c0:c0 + n_chunk] = jnp.dot(
            h, w_ref[:, c0:c0 + n_chunk], preferred_element_type=F32).astype(o_ref.dtype)
    if has_aux:
        aux_ref[0] = jnp.dot(h, waux_ref[...], preferred_element_type=F32)


def _norm_proj(x, sc, sh, w, w_aux=None, *, tm=512, n_chunk=1024):
    b, s, d = x.shape
    n = w.shape[1]
    tm = min(tm, s)
    in_specs = [pl.BlockSpec((1, tm, d), lambda bi, i: (bi, i, 0)),
                pl.BlockSpec((1, 1, d), lambda bi, i: (bi, 0, 0)),
                pl.BlockSpec((1, 1, d), lambda bi, i: (bi, 0, 0)),
                _resident((d, n))]
    out_specs = [pl.BlockSpec((1, tm, n), lambda bi, i: (bi, i, 0))]
    out_shape = [jax.ShapeDtypeStruct((b, s, n), BF16)]
    args = [x, sc, sh, w]
    if w_aux is not None:
        na = w_aux.shape[1]
        in_specs.append(_resident((d, na)))
        out_specs.append(pl.BlockSpec((1, tm, na), lambda bi, i: (bi, i, 0)))
        out_shape.append(jax.ShapeDtypeStruct((b, s, na), F32))
        args.append(w_aux)
    out = pl.pallas_call(
        functools.partial(_norm_proj_kernel, n_chunk=n_chunk, has_aux=w_aux is not None),
        grid=(b, s // tm),
        in_specs=in_specs, out_specs=out_specs, out_shape=out_shape,
        compiler_params=_cparams(("parallel", "parallel")),
        name="norm_proj",
    )(*args)
    return out if w_aux is not None else out[0]


def _proj_res_kernel(a_ref, w_ref, x_ref, g_ref, o_ref):
    o_ref[0] = x_ref[0] + g_ref[0] * jnp.dot(a_ref[0], w_ref[...], preferred_element_type=F32)


def _proj_res(a, w, x, g, *, tm=512):
    b, s, k = a.shape
    d = w.shape[1]
    tm = min(tm, s)
    return pl.pallas_call(
        _proj_res_kernel,
        grid=(b, s // tm),
        in_specs=[pl.BlockSpec((1, tm, k), lambda bi, i: (bi, i, 0)),
                  _resident((k, d)),
                  pl.BlockSpec((1, tm, d), lambda bi, i: (bi, i, 0)),
                  pl.BlockSpec((1, 1, d), lambda bi, i: (bi, 0, 0))],
        out_specs=pl.BlockSpec((1, tm, d), lambda bi, i: (bi, i, 0)),
        out_shape=jax.ShapeDtypeStruct((b, s, d), F32),
        compiler_params=_cparams(("parallel", "parallel")),
        name="proj_res",
    )(a, w, x, g)


def _softplus(v):
    return jnp.maximum(v, 0.0) + jnp.log1p(jnp.exp(-jnp.abs(v)))


def _ssd_kernel(z_ref, x_ref, bc_ref, dt_ref, convw_ref, convb_ref, dtb_ref, alog_ref, dskip_ref,
                ng_ref, expand_ref, y_ref, win_ref, state_ref):
    q = M_CHUNK
    gw = M_GROUP_W

    @pl.when(pl.program_id(1) == 0)
    def _():
        win_ref[0:8, :] = jnp.zeros((8, M_CONV_DIM), F32)
        state_ref[...] = jnp.zeros_like(state_ref)

    win_ref[8:8 + q, 0:M_D_INNER] = x_ref[0].astype(F32)
    win_ref[8:8 + q, M_D_INNER:M_CONV_DIM] = bc_ref[0].astype(F32)
    cw = convw_ref[...]
    acc = convb_ref[...] + cw[M_CONV - 1:M_CONV, :] * win_ref[8:8 + q, :]
    for k in range(M_CONV - 1):
        acc = acc + cw[k:k + 1, :] * win_ref[5 + k:5 + k + q, :]
    win_ref[0:8, :] = win_ref[q:q + 8, :]
    xbc = _silu(acc)

    dt = _softplus(dt_ref[0] + dtb_ref[...])
    da = dt * (-jnp.exp(alog_ref[...]))
    ri = lax.broadcasted_iota(jnp.int32, (q, q), 0)
    ci = lax.broadcasted_iota(jnp.int32, (q, q), 1)
    tril = ci <= ri
    acs = jnp.dot(tril.astype(F32), da, preferred_element_type=F32, precision=HIGHEST)
    acs_t = acs.T
    dt_t = dt.T
    a_last = acs[q - 1:q, :]

    expand = expand_ref[...]

    def per_channel(v):
        hi = v.astype(BF16)
        lo = (v - hi.astype(F32)).astype(BF16)
        return (jnp.dot(hi, expand, preferred_element_type=F32)
                + jnp.dot(lo, expand, preferred_element_type=F32))

    decay_in = per_channel(jnp.exp(acs))
    decay_out_dt = per_channel(jnp.exp(a_last - acs) * dt)
    decay_chunk = decay_in[q - 1:q, :]

    lane = lax.broadcasted_iota(jnp.int32, (q, LANES), 1)
    for g in range(M_GROUPS):
        b16 = xbc[:, M_D_INNER + g * M_STATE:M_D_INNER + (g + 1) * M_STATE].astype(BF16)
        c16 = xbc[:, M_D_INNER + (M_GROUPS + g) * M_STATE:
                  M_D_INNER + (M_GROUPS + g + 1) * M_STATE].astype(BF16)
        xs = xbc[:, g * gw:(g + 1) * gw]
        xs16 = xs.astype(BF16)
        cb = lax.dot_general(c16, b16, (((1,), (1,)), ((), ())), preferred_element_type=F32)
        state = state_ref[g]
        y_off = (jnp.dot(c16, state.astype(BF16), preferred_element_type=F32)
                 * decay_in[:, g * gw:(g + 1) * gw])
        pairs = []
        for p in range(M_HPG // 2):
            ms = []
            for r in range(2):
                h = g * M_HPG + 2 * p + r
                seg = acs[:, h:h + 1] - acs_t[h:h + 1, :]
                decay = jnp.exp(jnp.where(tril, seg, -jnp.inf))
                ms.append((cb * decay * dt_t[h:h + 1, :]).astype(BF16))
            x2 = xs16[:, p * LANES:(p + 1) * LANES]
            zero = jnp.zeros_like(x2)
            rhs = jnp.concatenate([jnp.where(lane < M_HEAD_DIM, x2, zero),
                                   jnp.where(lane >= M_HEAD_DIM, x2, zero)], axis=0)
            pairs.append(jnp.dot(jnp.concatenate(ms, axis=1), rhs, preferred_element_type=F32))
        y = jnp.concatenate(pairs, axis=1) + y_off + xs * dskip_ref[:, g * gw:(g + 1) * gw]

        xw = (xs * decay_out_dt[:, g * gw:(g + 1) * gw]).astype(BF16)
        state_ref[g] = (state * decay_chunk[:, g * gw:(g + 1) * gw]
                        + lax.dot_general(b16, xw, (((0,), (0,)), ((), ())),
                                          preferred_element_type=F32))

        y = y * _silu(z_ref[0, :, g * gw:(g + 1) * gw].astype(F32))
        ms2 = jnp.mean(y * y, axis=-1, keepdims=True)
        y_ref[0, :, g * gw:(g + 1) * gw] = (
            y * lax.rsqrt(ms2 + EPS) * ng_ref[:, g * gw:(g + 1) * gw]).astype(BF16)


def _ssd(main, dt_raw, conv_w, conv_b, dt_bias, a_log, d_skip, norm_g):
    b, s, _ = main.shape
    nc = s // M_CHUNK
    pad = LANES - M_HEADS
    head = lax.broadcasted_iota(jnp.int32, (LANES, M_D_INNER), 0)
    chan = lax.broadcasted_iota(jnp.int32, (LANES, M_D_INNER), 1)
    expand = (chan // M_HEAD_DIM == head).astype(BF16)
    vec = lambda n: pl.BlockSpec((1, n), lambda bi, c: (0, 0))
    blk = lambda j: pl.BlockSpec((1, M_CHUNK, M_D_INNER), lambda bi, c: (bi, c, j))
    return pl.pallas_call(
        _ssd_kernel,
        grid=(b, nc),
        in_specs=[blk(0), blk(1), blk(2),
                  pl.BlockSpec((1, M_CHUNK, LANES), lambda bi, c: (bi, c, 0)),
                  pl.BlockSpec((M_CONV, M_CONV_DIM), lambda bi, c: (0, 0)),
                  vec(M_CONV_DIM), vec(LANES), vec(LANES), vec(M_D_INNER), vec(M_D_INNER),
                  pl.BlockSpec((LANES, M_D_INNER), lambda bi, c: (0, 0))],
        out_specs=pl.BlockSpec((1, M_CHUNK, M_D_INNER), lambda bi, c: (bi, c, 0)),
        out_shape=jax.ShapeDtypeStruct((b, s, M_D_INNER), BF16),
        scratch_shapes=[pltpu.VMEM((M_CHUNK + 8, M_CONV_DIM), F32),
                        pltpu.VMEM((M_GROUPS, M_STATE, M_GROUP_W), F32)],
        compiler_params=_cparams(("parallel", "arbitrary")),
        name="ssd",
    )(main, main, main, dt_raw, conv_w, conv_b[None, :],
      jnp.pad(dt_bias, (0, pad))[None, :], jnp.pad(a_log, (0, pad))[None, :],
      jnp.repeat(d_skip, M_HEAD_DIM)[None, :], norm_g[None, :], expand)


def _swiglu_kernel(x_ref, sc_ref, sh_ref, g_ref, wg_ref, wu_ref, wd_ref, o_ref, *, f_chunk):
    x = x_ref[0]
    h = _norm_mod(x, sc_ref[0], sh_ref[0]).astype(BF16)
    f = wg_ref.shape[1]
    acc = jnp.zeros(x.shape, F32)
    for c0 in range(0, f, f_chunk):
        c1 = min(c0 + f_chunk, f)
        gate = jnp.dot(h, wg_ref[:, c0:c1], preferred_element_type=F32)
        up = jnp.dot(h, wu_ref[:, c0:c1], preferred_element_type=F32)
        acc = acc + jnp.dot((_silu(gate) * up).astype(BF16), wd_ref[c0:c1, :],
                            preferred_element_type=F32)
    o_ref[0] = x + g_ref[0] * acc


def _swiglu(x, sc, sh, g, wg, wu, wd, *, tm=512, f_chunk=512):
    b, s, d = x.shape
    f = wg.shape[1]
    tm = min(tm, s)
    mod = pl.BlockSpec((1, 1, d), lambda bi, i: (bi, 0, 0))
    return pl.pallas_call(
        functools.partial(_swiglu_kernel, f_chunk=f_chunk),
        grid=(b, s // tm),
        in_specs=[pl.BlockSpec((1, tm, d), lambda bi, i: (bi, i, 0)), mod, mod, mod,
                  _resident((d, f)), _resident((d, f)), _resident((f, d))],
        out_specs=pl.BlockSpec((1, tm, d), lambda bi, i: (bi, i, 0)),
        out_shape=jax.ShapeDtypeStruct((b, s, d), F32),
        compiler_params=_cparams(("parallel", "parallel")),
        name="swiglu",
    )(x, sc, sh, g, wg, wu, wd)


def _attn_kernel(q_ref, k_ref, v_ref, lam_ref, g_ref, o_ref, m_ref, l_ref, acc_ref, *,
                 tq, tk, lambda_init):
    qi = pl.program_id(2)
    lane = lax.broadcasted_iota(jnp.int32, (tq, LANES), 1)
    q = q_ref[0]
    q = (q.astype(F32) * (A_HEAD_DIM ** -0.5)).astype(BF16)
    zero = jnp.zeros_like(q)
    qq = jnp.concatenate([jnp.where(lane < A_HEAD_DIM, q, zero),
                          jnp.where(lane >= A_HEAD_DIM, q, zero)], axis=0)
    m_ref[...] = jnp.full(m_ref.shape, -jnp.inf, F32)
    l_ref[...] = jnp.zeros(l_ref.shape, F32)
    acc_ref[...] = jnp.zeros(acc_ref.shape, F32)

    def step(ki, masked):
        start = pl.multiple_of(ki * tk, tk)
        kb = k_ref[0, pl.ds(start, tk), :]
        vb = v_ref[0, pl.ds(start, tk), :]
        s = lax.dot_general(qq, kb, (((1,), (1,)), ((), ())), preferred_element_type=F32)
        if masked:
            row = qi * tq + lax.rem(lax.broadcasted_iota(jnp.int32, (2 * tq, tk), 0), tq)
            col = start + lax.broadcasted_iota(jnp.int32, (2 * tq, tk), 1)
            s = jnp.where(col <= row, s, -jnp.inf)
        m_old = m_ref[...]
        m_new = jnp.maximum(m_old, jnp.max(s, axis=-1, keepdims=True))
        alpha = jnp.exp(m_old - m_new)
        p = jnp.exp(s - m_new)
        l_ref[...] = alpha * l_ref[...] + jnp.sum(p, axis=-1, keepdims=True)
        acc_ref[...] = alpha * acc_ref[...] + jnp.dot(p.astype(BF16), vb,
                                                      preferred_element_type=F32)
        m_ref[...] = m_new

    n_full = (qi * tq) // tk

    def body(ki, carry):
        step(ki, False)
        return carry

    lax.fori_loop(0, n_full, body, 0)
    step(n_full, True)

    o = acc_ref[...] / l_ref[...]
    lam_v = lam_ref[...]
    lam = (jnp.exp(jnp.sum(lam_v[0:1] * lam_v[1:2], axis=-1, keepdims=True))
           - jnp.exp(jnp.sum(lam_v[2:3] * lam_v[3:4], axis=-1, keepdims=True)) + lambda_init)
    o = o[:tq] - lam * o[tq:]
    o = o * lax.rsqrt(jnp.mean(o * o, axis=-1, keepdims=True) + EPS)
    o_ref[0] = (o * g_ref[...] * (1.0 - lambda_init)).astype(BF16)


def _attention(qkv, lam_pack, subln_g, lambda_init):
    b, s, _ = qkv.shape
    tq = min(A_TQ, s)
    tk = min(A_TK, s)
    return pl.pallas_call(
        functools.partial(_attn_kernel, tq=tq, tk=tk, lambda_init=lambda_init),
        grid=(b, A_HEADS, s // tq),
        in_specs=[pl.BlockSpec((1, tq, LANES), lambda bi, h, i: (bi, i, h)),
                  pl.BlockSpec((1, s, LANES), lambda bi, h, i: (bi, 0, A_HEADS + h)),
                  pl.BlockSpec((1, s, LANES), lambda bi, h, i: (bi, 0, 2 * A_HEADS + h)),
                  pl.BlockSpec((8, LANES), lambda bi, h, i: (0, 0)),
                  pl.BlockSpec((1, LANES), lambda bi, h, i: (0, 0))],
        out_specs=pl.BlockSpec((1, tq, LANES), lambda bi, h, i: (bi, i, h)),
        out_shape=jax.ShapeDtypeStruct((b, s, A_HEADS * A_V_DIM), BF16),
        scratch_shapes=[pltpu.VMEM((2 * tq, 1), F32), pltpu.VMEM((2 * tq, 1), F32),
                        pltpu.VMEM((2 * tq, A_V_DIM), F32)],
        compiler_params=_cparams(("parallel", "parallel", "arbitrary")),
        name="diff_attn",
    )(qkv, qkv, qkv, lam_pack, subln_g[None, :])


def _router_kernel(x_ref, sc_ref, sh_ref, wr_ref, h_ref, info_ref, cnt_ref, run_ref):
    tm = x_ref.shape[1]

    @pl.when((pl.program_id(0) == 0) & (pl.program_id(1) == 0))
    def _():
        run_ref[...] = jnp.zeros_like(run_ref)

    h = _norm_mod(x_ref[0], sc_ref[0], sh_ref[0])
    h_ref[...] = h
    logits = jnp.dot(h, wr_ref[...], preferred_element_type=F32, precision=HIGHEST)
    lane = lax.broadcasted_iota(jnp.int32, (tm, LANES), 1)
    lg = jnp.where(lane < N_EXPERTS, logits, -jnp.inf)
    m0 = jnp.max(lg, axis=-1, keepdims=True)
    e0 = jnp.min(jnp.where(lg == m0, lane, LANES), axis=-1, keepdims=True)
    lg1 = jnp.where(lane == e0, -jnp.inf, lg)
    m1 = jnp.max(lg1, axis=-1, keepdims=True)
    e1 = jnp.min(jnp.where(lg1 == m1, lane, LANES), axis=-1, keepdims=True)
    ex = jnp.exp(m1 - m0)
    g0 = 1.0 / (1.0 + ex)
    g1 = ex / (1.0 + ex)

    pick0 = lane == e0
    pick1 = lane == e1
    onehot = (pick0 | pick1).astype(BF16)
    ri = lax.broadcasted_iota(jnp.int32, (tm, tm), 0)
    ci = lax.broadcasted_iota(jnp.int32, (tm, tm), 1)
    before = jnp.dot((ci < ri).astype(BF16), onehot, preferred_element_type=F32) + run_ref[0:1, :]
    rank0 = jnp.sum(jnp.where(pick0, before, 0.0), axis=-1, keepdims=True)
    rank1 = jnp.sum(jnp.where(pick1, before, 0.0), axis=-1, keepdims=True)
    total = run_ref[0:1, :] + jnp.sum(onehot.astype(F32), axis=0, keepdims=True)
    run_ref[...] = jnp.broadcast_to(total, run_ref.shape)
    cnt_ref[...] = jnp.broadcast_to(total, cnt_ref.shape)

    info = jnp.zeros((tm, LANES), F32)
    for i, v in enumerate((e0.astype(F32), e1.astype(F32), g0, g1, rank0, rank1)):
        info = jnp.where(lane == i, v, info)
    info_ref[...] = info


def _router(x, sc, sh, w_router, *, tm=256):
    b, s, d = x.shape
    tm = min(tm, s)
    nt = s // tm
    mod = pl.BlockSpec((1, 1, d), lambda bi, i: (bi, 0, 0))
    return pl.pallas_call(
        _router_kernel,
        grid=(b, nt),
        in_specs=[pl.BlockSpec((1, tm, d), lambda bi, i: (bi, i, 0)), mod, mod,
                  pl.BlockSpec((d, LANES), lambda bi, i: (0, 0))],
        out_specs=[pl.BlockSpec((tm, d), lambda bi, i: (bi * nt + i, 0)),
                   pl.BlockSpec((tm, LANES), lambda bi, i: (bi * nt + i, 0)),
                   pl.BlockSpec((8, LANES), lambda bi, i: (0, 0))],
        out_shape=[jax.ShapeDtypeStruct((b * s, d), F32),
                   jax.ShapeDtypeStruct((b * s, LANES), F32),
                   jax.ShapeDtypeStruct((8, LANES), F32)],
        scratch_shapes=[pltpu.VMEM((8, LANES), F32)],
        compiler_params=_cparams(("arbitrary", "arbitrary")),
        name="moe_router",
    )(x, sc, sh, jnp.pad(w_router, ((0, 0), (0, LANES - N_EXPERTS))))


def _row_copy(src_ref, src_row, dst_ref, dst_row, sem):
    return pltpu.make_async_copy(src_ref.at[pl.ds(src_row, 1)], dst_ref.at[pl.ds(dst_row, 1)], sem)


def _dispatch_kernel(dest_ref, h_ref, init_ref, xs_ref, sem):
    del init_ref
    tm = h_ref.shape[0]

    def issue(i, carry):
        for k in range(TOP_K):
            _row_copy(h_ref, i, xs_ref, dest_ref[TOP_K * i + k], sem).start()
        return carry

    def drain(i, carry):
        for k in range(TOP_K):
            _row_copy(h_ref, 0, xs_ref, 0, sem).wait()
        return carry

    lax.fori_loop(0, tm, issue, 0)
    lax.fori_loop(0, tm, drain, 0)


def _dispatch(h, dest, n_rows, *, tm=256):
    t, d = h.shape
    tm = min(tm, t)
    return pl.pallas_call(
        _dispatch_kernel,
        grid=(t // tm,),
        in_specs=[pl.BlockSpec((TOP_K * tm,), lambda i: (i,), memory_space=pltpu.SMEM),
                  pl.BlockSpec((tm, d), lambda i: (i, 0)),
                  pl.BlockSpec(memory_space=pl.ANY)],
        out_specs=pl.BlockSpec(memory_space=pl.ANY),
        out_shape=jax.ShapeDtypeStruct((n_rows, d), F32),
        scratch_shapes=[pltpu.SemaphoreType.DMA],
        input_output_aliases={2: 0},
        compiler_params=_cparams(("arbitrary",)),
        name="moe_dispatch",
    )(dest, h, jnp.zeros((n_rows, d), F32))


def _expert_kernel(be_ref, bv_ref, xs_ref, wg_ref, wu_ref, wd_ref, y_ref, *, f_chunk):
    del be_ref
    i = pl.program_id(0)

    @pl.when(bv_ref[i] != 0)
    def _():
        x = xs_ref[...].astype(BF16)
        f = wg_ref.shape[2]
        acc = jnp.zeros(y_ref.shape, F32)
        for c0 in range(0, f, f_chunk):
            gate = jnp.dot(x, wg_ref[0, :, c0:c0 + f_chunk], preferred_element_type=F32)
            up = jnp.dot(x, wu_ref[0, :, c0:c0 + f_chunk], preferred_element_type=F32)
            acc = acc + jnp.dot((_silu(gate) * up).astype(BF16), wd_ref[0, c0:c0 + f_chunk, :],
                                preferred_element_type=F32)
        y_ref[...] = acc

    @pl.when(bv_ref[i] == 0)
    def _():
        y_ref[...] = jnp.zeros_like(y_ref)


def _experts(xs, block_exp, block_valid, wg, wu, wd, *, f_chunk=512):
    n_rows, d = xs.shape
    f = wg.shape[2]
    n_blocks = n_rows // MOE_ROWS
    wspec = lambda shape: pl.BlockSpec(shape, lambda i, be, bv: (be[i], 0, 0),
                                       pipeline_mode=pl.Buffered(1))
    return pl.pallas_call(
        functools.partial(_expert_kernel, f_chunk=f_chunk),
        grid_spec=pltpu.PrefetchScalarGridSpec(
            num_scalar_prefetch=2,
            grid=(n_blocks,),
            in_specs=[pl.BlockSpec((MOE_ROWS, d), lambda i, be, bv: (i, 0)),
                      wspec((1, d, f)), wspec((1, d, f)), wspec((1, f, d))],
            out_specs=pl.BlockSpec((MOE_ROWS, d), lambda i, be, bv: (i, 0))),
        out_shape=jax.ShapeDtypeStruct((n_rows, d), F32),
        compiler_params=_cparams(("arbitrary",)),
        name="moe_experts",
    )(block_exp, block_valid, xs, wg, wu, wd)


def _combine_kernel(dest_ref, info_ref, x_ref, g_ref, fg_ref, y_ref, o_ref, buf_ref, sem):
    tm = x_ref.shape[0]

    def issue(i, carry):
        for k in range(TOP_K):
            _row_copy(y_ref, dest_ref[TOP_K * i + k], buf_ref.at[k], i, sem).start()
        return carry

    def drain(i, carry):
        for k in range(TOP_K):
            _row_copy(y_ref, 0, buf_ref.at[k], 0, sem).wait()
        return carry

    lax.fori_loop(0, tm, issue, 0)
    lax.fori_loop(0, tm, drain, 0)
    info = info_ref[...]
    ffn = info[:, 2:3] * buf_ref[0] + info[:, 3:4] * buf_ref[1]
    x = x_ref[...] + g_ref[0] * ffn
    o_ref[...] = x * lax.rsqrt(jnp.mean(x * x, axis=-1, keepdims=True) + EPS) * fg_ref[...]


def _combine(dest, info, x2d, g, final_g, y, seq, *, tm=256):
    t, d = x2d.shape
    tm = min(tm, seq)
    per_seq = seq // tm
    return pl.pallas_call(
        _combine_kernel,
        grid=(t // tm,),
        in_specs=[pl.BlockSpec((TOP_K * tm,), lambda i: (i,), memory_space=pltpu.SMEM),
                  pl.BlockSpec((tm, LANES), lambda i: (i, 0)),
                  pl.BlockSpec((tm, d), lambda i: (i, 0)),
                  pl.BlockSpec((1, 1, d), lambda i: (i // per_seq, 0, 0)),
                  pl.BlockSpec((1, d), lambda i: (0, 0)),
                  pl.BlockSpec(memory_space=pl.ANY)],
        out_specs=pl.BlockSpec((tm, d), lambda i: (i, 0)),
        out_shape=jax.ShapeDtypeStruct((t, d), F32),
        scratch_shapes=[pltpu.VMEM((TOP_K, tm, d), F32), pltpu.SemaphoreType.DMA],
        compiler_params=_cparams(("arbitrary",)),
        name="moe_combine",
    )(dest, info, x2d, g, final_g[None, :], y)


def _moe_plan(info, counts):
    e = info[:, 0:TOP_K].astype(jnp.int32)
    rank = info[:, 4:4 + TOP_K].astype(jnp.int32)
    cnt = counts[0, :N_EXPERTS].astype(jnp.int32)
    padded = (cnt + MOE_ROWS - 1) // MOE_ROWS * MOE_ROWS
    pends = jnp.cumsum(padded)
    pstarts = pends - padded
    start_of = jnp.zeros_like(e)
    for j in range(N_EXPERTS):
        start_of = jnp.where(e == j, pstarts[j], start_of)
    dest = (start_of + rank).reshape(-1)
    n_blocks = (TOP_K * info.shape[0]) // MOE_ROWS + N_EXPERTS
    block_start = jnp.arange(n_blocks, dtype=jnp.int32) * MOE_ROWS
    valid = block_start < pends[-1]
    last_start = jnp.maximum(pends[-1] - MOE_ROWS, 0)
    block_exp = jnp.searchsorted(pends, jnp.where(valid, block_start, last_start), side='right')
    block_exp = jnp.minimum(block_exp, N_EXPERTS - 1).astype(jnp.int32)
    return dest, block_exp, valid.astype(jnp.int32), n_blocks * MOE_ROWS


def kernel(x, c, ada_w0, ada_b0, m_w_in, m_conv_w, m_conv_b, m_dt_bias, m_a_log, m_d_skip, m_norm_g, m_w_out, ffn_w_gate, ffn_w_up, ffn_w_down, ada_w1, ada_b1, a_w_qkv, a_lam_q1, a_lam_k1, a_lam_q2, a_lam_k2, a_subln_g, a_w_o, moe_w_router, moe_w_gate, moe_w_up, moe_w_down, final_g):
    b, s, d = x.shape
    bf = lambda w: w.astype(BF16)
    c8 = jnp.pad(c, ((0, 8 - b), (0, 0)))

    def mods(w, bias):
        mod = _adaln(c8, w, bias[None, :])[:b]
        return [m[:, None, :] for m in jnp.split(mod, 6, axis=-1)]

    sh1, sc1, g1, sh2, sc2, g2 = mods(ada_w0, ada_b0)
    w_dt = jnp.pad(m_w_in[:, M_MAIN_W:], ((0, 0), (0, LANES - M_HEADS)))
    main, dt_raw = _norm_proj(x, sc1, sh1, bf(m_w_in[:, :M_MAIN_W]), bf(w_dt))
    y = _ssd(main, dt_raw, m_conv_w, m_conv_b, m_dt_bias, m_a_log, m_d_skip, m_norm_g)
    x = _proj_res(y, bf(m_w_out), x, g1)
    x = _swiglu(x, sc2, sh2, g2, bf(ffn_w_gate), bf(ffn_w_up), bf(ffn_w_down))

    sh1, sc1, g1, sh2, sc2, g2 = mods(ada_w1, ada_b1)
    lambda_init = 0.8 - 0.6 * math.exp(-0.3 * 1)
    qkv = _norm_proj(x, sc1, sh1, bf(a_w_qkv))
    lam_pack = jnp.pad(jnp.stack([a_lam_q1, a_lam_k1, a_lam_q2, a_lam_k2]),
                       ((0, 4), (0, LANES - A_HEAD_DIM)))
    att = _attention(qkv, lam_pack, a_subln_g, lambda_init)
    x = _proj_res(att, bf(a_w_o), x, g1)

    h, info, counts = _router(x, sc2, sh2, moe_w_router)
    dest, block_exp, block_valid, n_rows = _moe_plan(info, counts)
    xs = _dispatch(h, dest, n_rows)
    ye = _experts(xs, block_exp, block_valid, bf(moe_w_gate), bf(moe_w_up), bf(moe_w_down))
    out = _combine(dest, info, x.reshape(b * s, d), g2, final_g, ye, s)
    return out.reshape(b, s, d)
```

```python
import functools
import math

import jax
import jax.numpy as jnp
from jax import lax
from jax.experimental import pallas as pl
from jax.experimental.pallas import tpu as pltpu

F32 = jnp.float32
BF16 = jnp.bfloat16
HIGHEST = lax.Precision.HIGHEST

D_MODEL = 1024
EPS = 1e-6
M_D_INNER = 2048
M_HEAD_DIM = 64
M_HEADS = 32
M_GROUPS = 8
M_HPG = 4
M_STATE = 128
M_CONV = 4
M_CHUNK = 128
M_GROUP_W = M_HPG * M_HEAD_DIM
M_BC_W = 2 * M_GROUPS * M_STATE
M_CONV_DIM = M_D_INNER + M_BC_W
M_MAIN_W = M_D_INNER + M_CONV_DIM
A_HEADS = 8
A_HEAD_DIM = 64
A_V_DIM = 128
A_TQ = 256
A_TK = 512
FFN_DIM = 2816
N_EXPERTS = 8
TOP_K = 2
EXPERT_DIM = 3584
MOE_ROWS = 512
LANES = 128
VMEM_LIMIT = 56 * 1024 * 1024


def _silu(v):
    return v * jax.nn.sigmoid(v)


def _norm_mod(x, sc, sh):
    ms = jnp.mean(x * x, axis=-1, keepdims=True)
    return x * lax.rsqrt(ms + EPS) * (1.0 + sc) + sh


def _cparams(sem):
    return pltpu.CompilerParams(dimension_semantics=sem, vmem_limit_bytes=VMEM_LIMIT)


def _resident(shape):
    nd = len(shape)
    return pl.BlockSpec(shape, lambda *_: (0,) * nd, pipeline_mode=pl.Buffered(1))


def _adaln_kernel(c_ref, w_ref, b_ref, o_ref):
    o_ref[...] = jnp.dot(_silu(c_ref[...]), w_ref[...], preferred_element_type=F32,
                         precision=HIGHEST) + b_ref[...]


def _adaln(c8, w, b):
    rows, d = c8.shape
    n = w.shape[1]
    tn = 1024
    return pl.pallas_call(
        _adaln_kernel,
        grid=(n // tn,),
        in_specs=[pl.BlockSpec((rows, d), lambda j: (0, 0)),
                  pl.BlockSpec((d, tn), lambda j: (0, j)),
                  pl.BlockSpec((1, tn), lambda j: (0, j))],
        out_specs=pl.BlockSpec((rows, tn), lambda j: (0, j)),
        out_shape=jax.ShapeDtypeStruct((rows, n), F32),
        compiler_params=_cparams(("arbitrary",)),
        name="adaln",
    )(c8, w, b)


def _norm_proj_kernel(x_ref, sc_ref, sh_ref, w_ref, *rest, n_chunk, has_aux):
    if has_aux:
        waux_ref, o_ref, aux_ref = rest
    else:
        (o_ref,) = rest
    h = _norm_mod(x_ref[0], sc_ref[0], sh_ref[0]).astype(BF16)
    n = w_ref.shape[1]
    for c0 in range(0, n, n_chunk):
        o_ref[0, :, c0:c0 + n_chunk] = jnp.dot(
            h, w_ref[:, c0:c0 + n_chunk], preferred_element_type=F32).astype(o_ref.dtype)
    if has_aux:
        aux_ref[0] = jnp.dot(h, waux_ref[...], preferred_element_type=F32)


def _norm_proj(x, sc, sh, w, w_aux=None, *, tm=512, n_chunk=1024):
    b, s, d = x.shape
    n = w.shape[1]
    tm = min(tm, s)
    in_specs = [pl.BlockSpec((1, tm, d), lambda bi, i: (bi, i, 0)),
                pl.BlockSpec((1, 1, d), lambda bi, i: (bi, 0, 0)),
                pl.BlockSpec((1, 1, d), lambda bi, i: (bi, 0, 0)),
                _resident((d, n))]
    out_specs = [pl.BlockSpec((1, tm, n), lambda bi, i: (bi, i, 0))]
    out_shape = [jax.ShapeDtypeStruct((b, s, n), BF16)]
    args = [x, sc, sh, w]
    if w_aux is not None:
        na = w_aux.shape[1]
        in_specs.append(_resident((d, na)))
        out_specs.append(pl.BlockSpec((1, tm, na), lambda bi, i: (bi, i, 0)))
        out_shape.append(jax.ShapeDtypeStruct((b, s, na), F32))
        args.append(w_aux)
    out = pl.pallas_call(
        functools.partial(_norm_proj_kernel, n_chunk=n_chunk, has_aux=w_aux is not None),
        grid=(b, s // tm),
        in_specs=in_specs, out_specs=out_specs, out_shape=out_shape,
        compiler_params=_cparams(("parallel", "parallel")),
        name="norm_proj",
    )(*args)
    return out if w_aux is not None else out[0]


def _proj_res_kernel(a_ref, w_ref, x_ref, g_ref, o_ref):
    o_ref[0] = x_ref[0] + g_ref[0] * jnp.dot(a_ref[0], w_ref[...], preferred_element_type=F32)


def _proj_res(a, w, x, g, *, tm=512):
    b, s, k = a.shape
    d = w.shape[1]
    tm = min(tm, s)
    return pl.pallas_call(
        _proj_res_kernel,
        grid=(b, s // tm),
        in_specs=[pl.BlockSpec((1, tm, k), lambda bi, i: (bi, i, 0)),
                  _resident((k, d)),
                  pl.BlockSpec((1, tm, d), lambda bi, i: (bi, i, 0)),
                  pl.BlockSpec((1, 1, d), lambda bi, i: (bi, 0, 0))],
        out_specs=pl.BlockSpec((1, tm, d), lambda bi, i: (bi, i, 0)),
        out_shape=jax.ShapeDtypeStruct((b, s, d), F32),
        compiler_params=_cparams(("parallel", "parallel")),
        name="proj_res",
    )(a, w, x, g)


def _softplus(v):
    return jnp.maximum(v, 0.0) + jnp.log1p(jnp.exp(-jnp.abs(v)))


def _ssd_kernel(z_ref, x_ref, bc_ref, dt_ref, convw_ref, convb_ref, dtb_ref, alog_ref, dskip_ref,
                ng_ref, expand_ref, y_ref, win_ref, state_ref):
    q = M_CHUNK
    gw = M_GROUP_W

    @pl.when(pl.program_id(1) == 0)
    def _():
        win_ref[0:8, :] = jnp.zeros((8, M_CONV_DIM), F32)
        state_ref[...] = jnp.zeros_like(state_ref)

    win_ref[8:8 + q, 0:M_D_INNER] = x_ref[0].astype(F32)
    win_ref[8:8 + q, M_D_INNER:M_CONV_DIM] = bc_ref[0].astype(F32)
    cw = convw_ref[...]
    acc = convb_ref[...] + cw[M_CONV - 1:M_CONV, :] * win_ref[8:8 + q, :]
    for k in range(M_CONV - 1):
        acc = acc + cw[k:k + 1, :] * win_ref[5 + k:5 + k + q, :]
    win_ref[0:8, :] = win_ref[q:q + 8, :]
    xbc = _silu(acc)

    dt = _softplus(dt_ref[0] + dtb_ref[...])
    da = dt * (-jnp.exp(alog_ref[...]))
    ri = lax.broadcasted_iota(jnp.int32, (q, q), 0)
    ci = lax.broadcasted_iota(jnp.int32, (q, q), 1)
    tril = ci <= ri
    acs = jnp.dot(tril.astype(F32), da, preferred_element_type=F32, precision=HIGHEST)
    acs_t = acs.T
    dt_t = dt.T
    a_last = acs[q - 1:q, :]

    expand = expand_ref[...]

    def per_channel(v):
        hi = v.astype(BF16)
        lo = (v - hi.astype(F32)).astype(BF16)
        return (jnp.dot(hi, expand, preferred_element_type=F32)
                + jnp.dot(lo, expand, preferred_element_type=F32))

    decay_in = per_channel(jnp.exp(acs))
    decay_out_dt = per_channel(jnp.exp(a_last - acs) * dt)
    decay_chunk = decay_in[q - 1:q, :]

    lane = lax.broadcasted_iota(jnp.int32, (q, LANES), 1)
    for g in range(M_GROUPS):
        b16 = xbc[:, M_D_INNER + g * M_STATE:M_D_INNER + (g + 1) * M_STATE].astype(BF16)
        c16 = xbc[:, M_D_INNER + (M_GROUPS + g) * M_STATE:
                  M_D_INNER + (M_GROUPS + g + 1) * M_STATE].astype(BF16)
        xs = xbc[:, g * gw:(g + 1) * gw]
        xs16 = xs.astype(BF16)
        cb = lax.dot_general(c16, b16, (((1,), (1,)), ((), ())), preferred_element_type=F32)
        state = state_ref[g]
        y_off = (jnp.dot(c16, state.astype(BF16), preferred_element_type=F32)
                 * decay_in[:, g * gw:(g + 1) * gw])
        pairs = []
        for p in range(M_HPG // 2):
            ms = []
            for r in range(2):
                h = g * M_HPG + 2 * p + r
                seg = acs[:, h:h + 1] - acs_t[h:h + 1, :]
                decay = jnp.exp(jnp.where(tril, seg, -jnp.inf))
                ms.append((cb * decay * dt_t[h:h + 1, :]).astype(BF16))
            x2 = xs16[:, p * LANES:(p + 1) * LANES]
            zero = jnp.zeros_like(x2)
            rhs = jnp.concatenate([jnp.where(lane < M_HEAD_DIM, x2, zero),
                                   jnp.where(lane >= M_HEAD_DIM, x2, zero)], axis=0)
            pairs.append(jnp.dot(jnp.concatenate(ms, axis=1), rhs, preferred_element_type=F32))
        y = jnp.concatenate(pairs, axis=1) + y_off + xs * dskip_ref[:, g * gw:(g + 1) * gw]

        xw = (xs * decay_out_dt[:, g * gw:(g + 1) * gw]).astype(BF16)
        state_ref[g] = (state * decay_chunk[:, g * gw:(g + 1) * gw]
                        + lax.dot_general(b16, xw, (((0,), (0,)), ((), ())),
                                          preferred_element_type=F32))

        y = y * _silu(z_ref[0, :, g * gw:(g + 1) * gw].astype(F32))
        ms2 = jnp.mean(y * y, axis=-1, keepdims=True)
        y_ref[0, :, g * gw:(g + 1) * gw] = (
            y * lax.rsqrt(ms2 + EPS) * ng_ref[:, g * gw:(g + 1) * gw]).astype(BF16)


def _ssd(main, dt_raw, conv_w, conv_b, dt_bias, a_log, d_skip, norm_g):
    b, s, _ = main.shape
    nc = s // M_CHUNK
    pad = LANES - M_HEADS
    head = lax.broadcasted_iota(jnp.int32, (LANES, M_D_INNER), 0)
    chan = lax.broadcasted_iota(jnp.int32, (LANES, M_D_INNER), 1)
    expand = (chan // M_HEAD_DIM == head).astype(BF16)
    vec = lambda n: pl.BlockSpec((1, n), lambda bi, c: (0, 0))
    blk = lambda j: pl.BlockSpec((1, M_CHUNK, M_D_INNER), lambda bi, c: (bi, c, j))
    return pl.pallas_call(
        _ssd_kernel,
        grid=(b, nc),
        in_specs=[blk(0), blk(1), blk(2),
                  pl.BlockSpec((1, M_CHUNK, LANES), lambda bi, c: (bi, c, 0)),
                  pl.BlockSpec((M_CONV, M_CONV_DIM), lambda bi, c: (0, 0)),
                  vec(M_CONV_DIM), vec(LANES), vec(LANES), vec(M_D_INNER), vec(M_D_INNER),
                  pl.BlockSpec((LANES, M_D_INNER), lambda bi, c: (0, 0))],
        out_specs=pl.BlockSpec((1, M_CHUNK, M_D_INNER), lambda bi, c: (bi, c, 0)),
        out_shape=jax.ShapeDtypeStruct((b, s, M_D_INNER), BF16),
        scratch_shapes=[pltpu.VMEM((M_CHUNK + 8, M_CONV_DIM), F32),
                        pltpu.VMEM((M_GROUPS, M_STATE, M_GROUP_W), F32)],
        compiler_params=_cparams(("parallel", "arbitrary")),
        name="ssd",
    )(main, main, main, dt_raw, conv_w, conv_b[None, :],
      jnp.pad(dt_bias, (0, pad))[None, :], jnp.pad(a_log, (0, pad))[None, :],
      jnp.repeat(d_skip, M_HEAD_DIM)[None, :], norm_g[None, :], expand)


def _swiglu_kernel(x_ref, sc_ref, sh_ref, g_ref, wg_ref, wu_ref, wd_ref, o_ref, *, f_chunk):
    x = x_ref[0]
    h = _norm_mod(x, sc_ref[0], sh_ref[0]).astype(BF16)
    f = wg_ref.shape[1]
    acc = jnp.zeros(x.shape, F32)
    for c0 in range(0, f, f_chunk):
        c1 = min(c0 + f_chunk, f)
        gate = jnp.dot(h, wg_ref[:, c0:c1], preferred_element_type=F32)
        up = jnp.dot(h, wu_ref[:, c0:c1], preferred_element_type=F32)
        acc = acc + jnp.dot((_silu(gate) * up).astype(BF16), wd_ref[c0:c1, :],
                            preferred_element_type=F32)
    o_ref[0] = x + g_ref[0] * acc


def _swiglu(x, sc, sh, g, wg, wu, wd, *, tm=512, f_chunk=512):
    b, s, d = x.shape
    f = wg.shape[1]
    tm = min(tm, s)
    mod = pl.BlockSpec((1, 1, d), lambda bi, i: (bi, 0, 0))
    return pl.pallas_call(
        functools.partial(_swiglu_kernel, f_chunk=f_chunk),
        grid=(b, s // tm),
        in_specs=[pl.BlockSpec((1, tm, d), lambda bi, i: (bi, i, 0)), mod, mod, mod,
                  _resident((d, f)), _resident((d, f)), _resident((f, d))],
        out_specs=pl.BlockSpec((1, tm, d), lambda bi, i: (bi, i, 0)),
        out_shape=jax.ShapeDtypeStruct((b, s, d), F32),
        compiler_params=_cparams(("parallel", "parallel")),
        name="swiglu",
    )(x, sc, sh, g, wg, wu, wd)


A_V_ROWS = A_V_DIM + 16


def _attn_kernel(q_ref, k_ref, v_ref, lam_ref, g_ref, o_ref, vt_ref, qm_ref, st_ref, p_ref, m_ref,
                 al_ref, acc_ref, *, tq, tk, lambda_init):
    qi = pl.program_id(2)

    @pl.when(qi == 0)
    def _():
        for j in range(vt_ref.shape[0]):
            vt_ref[j, 0:A_V_DIM, :] = v_ref[0, j * tk:(j + 1) * tk, :].astype(F32).T.astype(BF16)
            vt_ref[j, A_V_DIM:A_V_ROWS, :] = jnp.ones((A_V_ROWS - A_V_DIM, tk), BF16)

    lane = lax.broadcasted_iota(jnp.int32, (tq, LANES), 1)
    q = q_ref[0]
    q = (q.astype(F32) * (A_HEAD_DIM ** -0.5)).astype(BF16)
    zero = jnp.zeros_like(q)
    qm_ref[0] = jnp.where(lane < A_HEAD_DIM, q, zero)
    qm_ref[1] = jnp.where(lane >= A_HEAD_DIM, q, zero)
    m_ref[...] = jnp.full(m_ref.shape, -jnp.inf, F32)
    acc_ref[...] = jnp.zeros(acc_ref.shape, F32)
    p_ref[...] = jnp.zeros(p_ref.shape, BF16)
    al_ref[...] = jnp.ones(al_ref.shape, F32)

    last = (qi * tq) // tk

    def scores(blk, j):
        start = pl.multiple_of(blk * tk, tk)
        st_ref[j] = lax.dot_general(k_ref[0, pl.ds(start, tk), :], qm_ref[j],
                                    (((1,), (1,)), ((), ())), preferred_element_type=F32)

    def softmax(blk, j, masked):
        st = st_ref[j]
        if masked:
            key = blk * tk + lax.broadcasted_iota(jnp.int32, (tk, tq), 0)
            qry = qi * tq + lax.broadcasted_iota(jnp.int32, (tk, tq), 1)
            st = jnp.where(key <= qry, st, -jnp.inf)
        m_old = m_ref[j]
        top = st
        while top.shape[0] > 32:
            half = top.shape[0] // 2
            top = jnp.maximum(top[:half], top[half:])
        m_new = jnp.maximum(m_old, jnp.max(top, axis=0, keepdims=True))
        al_ref[j] = jnp.exp(m_old - m_new)
        p_ref[j] = jnp.exp(st - m_new).astype(BF16)
        m_ref[j] = m_new

    def weighted_values(blk, j):
        acc_ref[j] = al_ref[j] * acc_ref[j] + jnp.dot(vt_ref[blk], p_ref[j],
                                                      preferred_element_type=F32)

    for j in range(2):
        scores(0, j)

    def visit(blk, diagonal):
        for j in range(2):
            weighted_values(jnp.maximum(blk - 1, 0), j)
            softmax(blk, j, diagonal)
            if not diagonal:
                scores(blk + 1, j)

    def body(blk, carry):
        visit(blk, False)
        return carry

    lax.fori_loop(0, last, body, 0)
    visit(last, True)
    for j in range(2):
        weighted_values(last, j)

    lam_v = lam_ref[...]
    lam = (jnp.exp(jnp.sum(lam_v[0:1] * lam_v[1:2], axis=-1, keepdims=True))
           - jnp.exp(jnp.sum(lam_v[2:3] * lam_v[3:4], axis=-1, keepdims=True)) + lambda_init)
    acc0 = acc_ref[0]
    acc1 = acc_ref[1]
    o = (acc0[0:A_V_DIM] / acc0[A_V_DIM:A_V_DIM + 1]
         - lam * (acc1[0:A_V_DIM] / acc1[A_V_DIM:A_V_DIM + 1])).T
    o = o * lax.rsqrt(jnp.mean(o * o, axis=-1, keepdims=True) + EPS)
    o_ref[0] = (o * g_ref[...] * (1.0 - lambda_init)).astype(BF16)


def _attention(qkv, lam_pack, subln_g, lambda_init):
    b, s, _ = qkv.shape
    tq = min(A_TQ, s)
    tk = min(A_TK, s)
    return pl.pallas_call(
        functools.partial(_attn_kernel, tq=tq, tk=tk, lambda_init=lambda_init),
        grid=(b, A_HEADS, s // tq),
        in_specs=[pl.BlockSpec((1, tq, LANES), lambda bi, h, i: (bi, i, h)),
                  pl.BlockSpec((1, s, LANES), lambda bi, h, i: (bi, 0, A_HEADS + h)),
                  pl.BlockSpec((1, s, LANES), lambda bi, h, i: (bi, 0, 2 * A_HEADS + h)),
                  pl.BlockSpec((8, LANES), lambda bi, h, i: (0, 0)),
                  pl.BlockSpec((1, LANES), lambda bi, h, i: (0, 0))],
        out_specs=pl.BlockSpec((1, tq, LANES), lambda bi, h, i: (bi, i, h)),
        out_shape=jax.ShapeDtypeStruct((b, s, A_HEADS * A_V_DIM), BF16),
        scratch_shapes=[pltpu.VMEM((s // tk, A_V_ROWS, tk), BF16),
                        pltpu.VMEM((2, tq, LANES), BF16),
                        pltpu.VMEM((2, tk, tq), F32),
                        pltpu.VMEM((2, tk, tq), BF16),
                        pltpu.VMEM((2, 1, tq), F32),
                        pltpu.VMEM((2, 1, tq), F32),
                        pltpu.VMEM((2, A_V_ROWS, tq), F32)],
        compiler_params=_cparams(("parallel", "parallel", "arbitrary")),
        name="diff_attn",
    )(qkv, qkv, qkv, lam_pack, subln_g[None, :])


def _router_kernel(x_ref, sc_ref, sh_ref, wr_ref, h_ref, info_ref, cnt_ref, run_ref):
    tm = x_ref.shape[1]

    @pl.when((pl.program_id(0) == 0) & (pl.program_id(1) == 0))
    def _():
        run_ref[...] = jnp.zeros_like(run_ref)

    h = _norm_mod(x_ref[0], sc_ref[0], sh_ref[0])
    h_ref[...] = h
    logits = jnp.dot(h, wr_ref[...], preferred_element_type=F32, precision=HIGHEST)
    lane = lax.broadcasted_iota(jnp.int32, (tm, LANES), 1)
    lg = jnp.where(lane < N_EXPERTS, logits, -jnp.inf)
    m0 = jnp.max(lg, axis=-1, keepdims=True)
    e0 = jnp.min(jnp.where(lg == m0, lane, LANES), axis=-1, keepdims=True)
    lg1 = jnp.where(lane == e0, -jnp.inf, lg)
    m1 = jnp.max(lg1, axis=-1, keepdims=True)
    e1 = jnp.min(jnp.where(lg1 == m1, lane, LANES), axis=-1, keepdims=True)
    ex = jnp.exp(m1 - m0)
    g0 = 1.0 / (1.0 + ex)
    g1 = ex / (1.0 + ex)

    pick0 = lane == e0
    pick1 = lane == e1
    onehot = (pick0 | pick1).astype(BF16)
    ri = lax.broadcasted_iota(jnp.int32, (tm, tm), 0)
    ci = lax.broadcasted_iota(jnp.int32, (tm, tm), 1)
    before = jnp.dot((ci < ri).astype(BF16), onehot, preferred_element_type=F32) + run_ref[0:1, :]
    rank0 = jnp.sum(jnp.where(pick0, before, 0.0), axis=-1, keepdims=True)
    rank1 = jnp.sum(jnp.where(pick1, before, 0.0), axis=-1, keepdims=True)
    total = run_ref[0:1, :] + jnp.sum(onehot.astype(F32), axis=0, keepdims=True)
    run_ref[...] = jnp.broadcast_to(total, run_ref.shape)
    cnt_ref[...] = jnp.broadcast_to(total, cnt_ref.shape)

    info = jnp.zeros((tm, LANES), F32)
    for i, v in enumerate((e0.astype(F32), e1.astype(F32), g0, g1, rank0, rank1)):
        info = jnp.where(lane == i, v, info)
    info_ref[...] = info


def _router(x, sc, sh, w_router, *, tm=256):
    b, s, d = x.shape
    tm = min(tm, s)
    nt = s // tm
    mod = pl.BlockSpec((1, 1, d), lambda bi, i: (bi, 0, 0))
    return pl.pallas_call(
        _router_kernel,
        grid=(b, nt),
        in_specs=[pl.BlockSpec((1, tm, d), lambda bi, i: (bi, i, 0)), mod, mod,
                  pl.BlockSpec((d, LANES), lambda bi, i: (0, 0))],
        out_specs=[pl.BlockSpec((tm, d), lambda bi, i: (bi * nt + i, 0)),
                   pl.BlockSpec((tm, LANES), lambda bi, i: (bi * nt + i, 0)),
                   pl.BlockSpec((8, LANES), lambda bi, i: (0, 0))],
        out_shape=[jax.ShapeDtypeStruct((b * s, d), F32),
                   jax.ShapeDtypeStruct((b * s, LANES), F32),
                   jax.ShapeDtypeStruct((8, LANES), F32)],
        scratch_shapes=[pltpu.VMEM((8, LANES), F32)],
        compiler_params=_cparams(("arbitrary", "arbitrary")),
        name="moe_router",
    )(x, sc, sh, jnp.pad(w_router, ((0, 0), (0, LANES - N_EXPERTS))))


def _row_copy(src_ref, src_row, dst_ref, dst_row, sem):
    return pltpu.make_async_copy(src_ref.at[pl.ds(src_row, 1)], dst_ref.at[pl.ds(dst_row, 1)], sem)


def _dispatch_kernel(dest_ref, h_ref, init_ref, xs_ref, sem):
    del init_ref
    tm = h_ref.shape[0]

    def issue(i, carry):
        for k in range(TOP_K):
            _row_copy(h_ref, i, xs_ref, dest_ref[TOP_K * i + k], sem).start()
        return carry

    def drain(i, carry):
        for k in range(TOP_K):
            _row_copy(h_ref, 0, xs_ref, 0, sem).wait()
        return carry

    lax.fori_loop(0, tm, issue, 0)
    lax.fori_loop(0, tm, drain, 0)


def _dispatch(h, dest, n_rows, *, tm=256):
    t, d = h.shape
    tm = min(tm, t)
    return pl.pallas_call(
        _dispatch_kernel,
        grid=(t // tm,),
        in_specs=[pl.BlockSpec((TOP_K * tm,), lambda i: (i,), memory_space=pltpu.SMEM),
                  pl.BlockSpec((tm, d), lambda i: (i, 0)),
                  pl.BlockSpec(memory_space=pl.ANY)],
        out_specs=pl.BlockSpec(memory_space=pl.ANY),
        out_shape=jax.ShapeDtypeStruct((n_rows, d), F32),
        scratch_shapes=[pltpu.SemaphoreType.DMA],
        input_output_aliases={2: 0},
        compiler_params=_cparams(("arbitrary",)),
        name="moe_dispatch",
    )(dest, h, jnp.zeros((n_rows, d), F32))


def _expert_kernel(be_ref, bv_ref, xs_ref, wg_ref, wu_ref, wd_ref, y_ref, *, f_chunk):
    del be_ref
    i = pl.program_id(0)

    @pl.when(bv_ref[i] != 0)
    def _():
        x = xs_ref[...].astype(BF16)
        f = wg_ref.shape[2]
        acc = jnp.zeros(y_ref.shape, F32)
        for c0 in range(0, f, f_chunk):
            gate = jnp.dot(x, wg_ref[0, :, c0:c0 + f_chunk], preferred_element_type=F32)
            up = jnp.dot(x, wu_ref[0, :, c0:c0 + f_chunk], preferred_element_type=F32)
            acc = acc + jnp.dot((_silu(gate) * up).astype(BF16), wd_ref[0, c0:c0 + f_chunk, :],
                                preferred_element_type=F32)
        y_ref[...] = acc

    @pl.when(bv_ref[i] == 0)
    def _():
        y_ref[...] = jnp.zeros_like(y_ref)


def _experts(xs, block_exp, block_valid, wg, wu, wd, *, f_chunk=512):
    n_rows, d = xs.shape
    f = wg.shape[2]
    n_blocks = n_rows // MOE_ROWS
    wspec = lambda shape: pl.BlockSpec(shape, lambda i, be, bv: (be[i], 0, 0),
                                       pipeline_mode=pl.Buffered(1))
    return pl.pallas_call(
        functools.partial(_expert_kernel, f_chunk=f_chunk),
        grid_spec=pltpu.PrefetchScalarGridSpec(
            num_scalar_prefetch=2,
            grid=(n_blocks,),
            in_specs=[pl.BlockSpec((MOE_ROWS, d), lambda i, be, bv: (i, 0)),
                      wspec((1, d, f)), wspec((1, d, f)), wspec((1, f, d))],
            out_specs=pl.BlockSpec((MOE_ROWS, d), lambda i, be, bv: (i, 0))),
        out_shape=jax.ShapeDtypeStruct((n_rows, d), F32),
        compiler_params=_cparams(("arbitrary",)),
        name="moe_experts",
    )(block_exp, block_valid, xs, wg, wu, wd)


def _combine_kernel(dest_ref, info_ref, x_ref, g_ref, fg_ref, y_ref, o_ref, buf_ref, sem):
    tm = x_ref.shape[0]

    def issue(i, carry):
        for k in range(TOP_K):
            _row_copy(y_ref, dest_ref[TOP_K * i + k], buf_ref.at[k], i, sem).start()
        return carry

    def drain(i, carry):
        for k in range(TOP_K):
            _row_copy(y_ref, 0, buf_ref.at[k], 0, sem).wait()
        return carry

    lax.fori_loop(0, tm, issue, 0)
    lax.fori_loop(0, tm, drain, 0)
    info = info_ref[...]
    ffn = info[:, 2:3] * buf_ref[0] + info[:, 3:4] * buf_ref[1]
    x = x_ref[...] + g_ref[0] * ffn
    o_ref[...] = x * lax.rsqrt(jnp.mean(x * x, axis=-1, keepdims=True) + EPS) * fg_ref[...]


def _combine(dest, info, x2d, g, final_g, y, seq, *, tm=256):
    t, d = x2d.shape
    tm = min(tm, seq)
    per_seq = seq // tm
    return pl.pallas_call(
        _combine_kernel,
        grid=(t // tm,),
        in_specs=[pl.BlockSpec((TOP_K * tm,), lambda i: (i,), memory_space=pltpu.SMEM),
                  pl.BlockSpec((tm, LANES), lambda i: (i, 0)),
                  pl.BlockSpec((tm, d), lambda i: (i, 0)),
                  pl.BlockSpec((1, 1, d), lambda i: (i // per_seq, 0, 0)),
                  pl.BlockSpec((1, d), lambda i: (0, 0)),
                  pl.BlockSpec(memory_space=pl.ANY)],
        out_specs=pl.BlockSpec((tm, d), lambda i: (i, 0)),
        out_shape=jax.ShapeDtypeStruct((t, d), F32),
        scratch_shapes=[pltpu.VMEM((TOP_K, tm, d), F32), pltpu.SemaphoreType.DMA],
        compiler_params=_cparams(("arbitrary",)),
        name="moe_combine",
    )(dest, info, x2d, g, final_g[None, :], y)


def _moe_plan(info, counts):
    e = info[:, 0:TOP_K].astype(jnp.int32)
    rank = info[:, 4:4 + TOP_K].astype(jnp.int32)
    cnt = counts[0, :N_EXPERTS].astype(jnp.int32)
    padded = (cnt + MOE_ROWS - 1) // MOE_ROWS * MOE_ROWS
    pends = jnp.cumsum(padded)
    pstarts = pends - padded
    start_of = jnp.zeros_like(e)
    for j in range(N_EXPERTS):
        start_of = jnp.where(e == j, pstarts[j], start_of)
    dest = (start_of + rank).reshape(-1)
    n_blocks = (TOP_K * info.shape[0]) // MOE_ROWS + N_EXPERTS
    block_start = jnp.arange(n_blocks, dtype=jnp.int32) * MOE_ROWS
    valid = block_start < pends[-1]
    last_start = jnp.maximum(pends[-1] - MOE_ROWS, 0)
    block_exp = jnp.searchsorted(pends, jnp.where(valid, block_start, last_start), side='right')
    block_exp = jnp.minimum(block_exp, N_EXPERTS - 1).astype(jnp.int32)
    return dest, block_exp, valid.astype(jnp.int32), n_blocks * MOE_ROWS


def kernel(x, c, ada_w0, ada_b0, m_w_in, m_conv_w, m_conv_b, m_dt_bias, m_a_log, m_d_skip, m_norm_g, m_w_out, ffn_w_gate, ffn_w_up, ffn_w_down, ada_w1, ada_b1, a_w_qkv, a_lam_q1, a_lam_k1, a_lam_q2, a_lam_k2, a_subln_g, a_w_o, moe_w_router, moe_w_gate, moe_w_up, moe_w_down, final_g):
    b, s, d = x.shape
    bf = lambda w: w.astype(BF16)
    c8 = jnp.pad(c, ((0, 8 - b), (0, 0)))

    def mods(w, bias):
        mod = _adaln(c8, w, bias[None, :])[:b]
        return [m[:, None, :] for m in jnp.split(mod, 6, axis=-1)]

    sh1, sc1, g1, sh2, sc2, g2 = mods(ada_w0, ada_b0)
    w_dt = jnp.pad(m_w_in[:, M_MAIN_W:], ((0, 0), (0, LANES - M_HEADS)))
    main, dt_raw = _norm_proj(x, sc1, sh1, bf(m_w_in[:, :M_MAIN_W]), bf(w_dt))
    y = _ssd(main, dt_raw, m_conv_w, m_conv_b, m_dt_bias, m_a_log, m_d_skip, m_norm_g)
    x = _proj_res(y, bf(m_w_out), x, g1)
    x = _swiglu(x, sc2, sh2, g2, bf(ffn_w_gate), bf(ffn_w_up), bf(ffn_w_down))

    sh1, sc1, g1, sh2, sc2, g2 = mods(ada_w1, ada_b1)
    lambda_init = 0.8 - 0.6 * math.exp(-0.3 * 1)
    qkv = _norm_proj(x, sc1, sh1, bf(a_w_qkv))
    lam_pack = jnp.pad(jnp.stack([a_lam_q1, a_lam_k1, a_lam_q2, a_lam_k2]),
                       ((0, 4), (0, LANES - A_HEAD_DIM)))
    att = _attention(qkv, lam_pack, a_subln_g, lambda_init)
    x = _proj_res(att, bf(a_w_o), x, g1)

    h, info, counts = _router(x, sc2, sh2, moe_w_router)
    dest, block_exp, block_valid, n_rows = _moe_plan(info, counts)
    xs = _dispatch(h, dest, n_rows)
    ye = _experts(xs, block_exp, block_valid, bf(moe_w_gate), bf(moe_w_up), bf(moe_w_down))
    out = _combine(dest, info, x.reshape(b * s, d), g2, final_g, ye, s)
    return out.reshape(b, s, d)
```

```python
import functools
import math

import jax
import jax.numpy as jnp
from jax import lax
from jax.experimental import pallas as pl
from jax.experimental.pallas import tpu as pltpu

F32 = jnp.float32
BF16 = jnp.bfloat16
HIGHEST = lax.Precision.HIGHEST

D_MODEL = 1024
EPS = 1e-6
M_D_INNER = 2048
M_HEAD_DIM = 64
M_HEADS = 32
M_GROUPS = 8
M_HPG = 4
M_STATE = 128
M_CONV = 4
M_CHUNK = 128
M_CONV_COLS = 512
M_WIN_PAD = 16
M_GROUP_W = M_HPG * M_HEAD_DIM
M_BC_W = 2 * M_GROUPS * M_STATE
M_CONV_DIM = M_D_INNER + M_BC_W
M_MAIN_W = M_D_INNER + M_CONV_DIM
A_HEADS = 8
A_HEAD_DIM = 64
A_V_DIM = 128
A_TQ = 512
A_TK = 512
FFN_DIM = 2816
N_EXPERTS = 8
TOP_K = 2
EXPERT_DIM = 3584
MOE_ROWS = 512
LANES = 128
VMEM_LIMIT = 56 * 1024 * 1024


def _silu(v):
    return v * jax.nn.sigmoid(v)


def _norm_mod(x, sc, sh):
    ms = jnp.mean(x * x, axis=-1, keepdims=True)
    return x * lax.rsqrt(ms + EPS) * (1.0 + sc) + sh


def _cparams(sem):
    return pltpu.CompilerParams(dimension_semantics=sem, vmem_limit_bytes=VMEM_LIMIT)


def _resident(shape):
    nd = len(shape)
    return pl.BlockSpec(shape, lambda *_: (0,) * nd, pipeline_mode=pl.Buffered(1))


def _adaln_kernel(c_ref, w_ref, b_ref, o_ref):
    o_ref[...] = jnp.dot(_silu(c_ref[...]), w_ref[...], preferred_element_type=F32,
                         precision=HIGHEST) + b_ref[...]


def _adaln(c8, w, b):
    rows, d = c8.shape
    n = w.shape[1]
    tn = 1024
    return pl.pallas_call(
        _adaln_kernel,
        grid=(n // tn,),
        in_specs=[pl.BlockSpec((rows, d), lambda j: (0, 0)),
                  pl.BlockSpec((d, tn), lambda j: (0, j)),
                  pl.BlockSpec((1, tn), lambda j: (0, j))],
        out_specs=pl.BlockSpec((rows, tn), lambda j: (0, j)),
        out_shape=jax.ShapeDtypeStruct((rows, n), F32),
        compiler_params=_cparams(("arbitrary",)),
        name="adaln",
    )(c8, w, b)


def _norm_proj_kernel(x_ref, sc_ref, sh_ref, w_ref, *rest, n_chunk, has_aux):
    if has_aux:
        waux_ref, o_ref, aux_ref = rest
    else:
        (o_ref,) = rest
    h = _norm_mod(x_ref[0], sc_ref[0], sh_ref[0]).astype(BF16)
    n = w_ref.shape[1]
    for c0 in range(0, n, n_chunk):
        o_ref[0, :, c0:c0 + n_chunk] = jnp.dot(
            h, w_ref[:, c0:c0 + n_chunk], preferred_element_type=F32).astype(o_ref.dtype)
    if has_aux:
        aux_ref[0] = jnp.dot(h, waux_ref[...], preferred_element_type=F32)


def _norm_proj(x, sc, sh, w, w_aux=None, *, tm=512, n_chunk=1024):
    b, s, d = x.shape
    n = w.shape[1]
    tm = min(tm, s)
    in_specs = [pl.BlockSpec((1, tm, d), lambda bi, i: (bi, i, 0)),
                pl.BlockSpec((1, 1, d), lambda bi, i: (bi, 0, 0)),
                pl.BlockSpec((1, 1, d), lambda bi, i: (bi, 0, 0)),
                _resident((d, n))]
    out_specs = [pl.BlockSpec((1, tm, n), lambda bi, i: (bi, i, 0))]
    out_shape = [jax.ShapeDtypeStruct((b, s, n), BF16)]
    args = [x, sc, sh, w]
    if w_aux is not None:
        na = w_aux.shape[1]
        in_specs.append(_resident((d, na)))
        out_specs.append(pl.BlockSpec((1, tm, na), lambda bi, i: (bi, i, 0)))
        out_shape.append(jax.ShapeDtypeStruct((b, s, na), F32))
        args.append(w_aux)
    out = pl.pallas_call(
        functools.partial(_norm_proj_kernel, n_chunk=n_chunk, has_aux=w_aux is not None),
        grid=(b, s // tm),
        in_specs=in_specs, out_specs=out_specs, out_shape=out_shape,
        compiler_params=_cparams(("parallel", "parallel")),
        name="norm_proj",
    )(*args)
    return out if w_aux is not None else out[0]


def _proj_res_kernel(a_ref, w_ref, x_ref, g_ref, o_ref):
    o_ref[0] = x_ref[0] + g_ref[0] * jnp.dot(a_ref[0], w_ref[...], preferred_element_type=F32)


def _proj_res(a, w, x, g, *, tm=512):
    b, s, k = a.shape
    d = w.shape[1]
    tm = min(tm, s)
    return pl.pallas_call(
        _proj_res_kernel,
        grid=(b, s // tm),
        in_specs=[pl.BlockSpec((1, tm, k), lambda bi, i: (bi, i, 0)),
                  _resident((k, d)),
                  pl.BlockSpec((1, tm, d), lambda bi, i: (bi, i, 0)),
                  pl.BlockSpec((1, 1, d), lambda bi, i: (bi, 0, 0))],
        out_specs=pl.BlockSpec((1, tm, d), lambda bi, i: (bi, i, 0)),
        out_shape=jax.ShapeDtypeStruct((b, s, d), F32),
        compiler_params=_cparams(("parallel", "parallel")),
        name="proj_res",
    )(a, w, x, g)


def _softplus(v):
    return jnp.maximum(v, 0.0) + jnp.log1p(jnp.exp(-jnp.abs(v)))


def _ssd_kernel(z_ref, x_ref, bc_ref, dt_ref, convw_ref, convb_ref, dtb_ref, alog_ref, dskip_ref,
                ng_ref, expand_ref, shift_ref, y_ref, win_ref, xbc_ref, state_ref):
    q = M_CHUNK
    gw = M_GROUP_W

    @pl.when(pl.program_id(1) == 0)
    def _():
        win_ref[0:M_WIN_PAD, :] = jnp.zeros((M_WIN_PAD, M_CONV_DIM), BF16)
        state_ref[...] = jnp.zeros_like(state_ref)

    win_ref[M_WIN_PAD:M_WIN_PAD + q, 0:M_D_INNER] = x_ref[0]
    win_ref[M_WIN_PAD:M_WIN_PAD + q, M_D_INNER:M_CONV_DIM] = bc_ref[0]
    shift = shift_ref[...]

    def conv_block(j):
        cols = slice(j * M_CONV_COLS, (j + 1) * M_CONV_COLS)
        shifted = jnp.dot(shift, win_ref[:, cols], preferred_element_type=F32)
        acc = (convb_ref[:, cols] + convw_ref[M_CONV - 1:M_CONV, cols]
               * win_ref[M_WIN_PAD:M_WIN_PAD + q, cols].astype(F32))
        for k in range(M_CONV - 1):
            acc = acc + convw_ref[k:k + 1, cols] * shifted[k * q:(k + 1) * q, :]
        xbc_ref[:, cols] = _silu(acc)

    for j in (4, 6, 0):
        conv_block(j)
    conv_after_group = {0: 1, 1: 5, 2: 7, 3: 2, 4: 3}

    dt = _softplus(dt_ref[0] + dtb_ref[...])
    da = dt * (-jnp.exp(alog_ref[...]))
    ri = lax.broadcasted_iota(jnp.int32, (q, q), 0)
    ci = lax.broadcasted_iota(jnp.int32, (q, q), 1)
    tril = ci <= ri
    acs = jnp.dot(tril.astype(F32), da, preferred_element_type=F32, precision=HIGHEST)
    acs_t = acs.T
    dt_t = dt.T
    a_last = acs[q - 1:q, :]

    expand = expand_ref[...]

    def per_channel(v):
        hi = v.astype(BF16)
        lo = (v - hi.astype(F32)).astype(BF16)
        return (jnp.dot(hi, expand, preferred_element_type=F32)
                + jnp.dot(lo, expand, preferred_element_type=F32))

    decay_in = per_channel(jnp.exp(acs))
    decay_out_dt = per_channel(jnp.exp(a_last - acs) * dt)
    decay_chunk = decay_in[q - 1:q, :]

    lane = lax.broadcasted_iota(jnp.int32, (q, LANES), 1)
    for g in range(M_GROUPS):
        b16 = xbc_ref[:, M_D_INNER + g * M_STATE:M_D_INNER + (g + 1) * M_STATE].astype(BF16)
        c16 = xbc_ref[:, M_D_INNER + (M_GROUPS + g) * M_STATE:
                  M_D_INNER + (M_GROUPS + g + 1) * M_STATE].astype(BF16)
        xs = xbc_ref[:, g * gw:(g + 1) * gw]
        xs16 = xs.astype(BF16)
        cb = lax.dot_general(c16, b16, (((1,), (1,)), ((), ())), preferred_element_type=F32)
        state = state_ref[g]
        y_off = (jnp.dot(c16, state.astype(BF16), preferred_element_type=F32)
                 * decay_in[:, g * gw:(g + 1) * gw])
        pairs = []
        for p in range(M_HPG // 2):
            ms = []
            for r in range(2):
                h = g * M_HPG + 2 * p + r
                seg = acs[:, h:h + 1] - acs_t[h:h + 1, :]
                decay = jnp.exp(jnp.where(tril, seg, -jnp.inf))
                ms.append((cb * decay * dt_t[h:h + 1, :]).astype(BF16))
            x2 = xs16[:, p * LANES:(p + 1) * LANES]
            zero = jnp.zeros_like(x2)
            rhs = jnp.concatenate([jnp.where(lane < M_HEAD_DIM, x2, zero),
                                   jnp.where(lane >= M_HEAD_DIM, x2, zero)], axis=0)
            pairs.append(jnp.dot(jnp.concatenate(ms, axis=1), rhs, preferred_element_type=F32))
        y = jnp.concatenate(pairs, axis=1) + y_off + xs * dskip_ref[:, g * gw:(g + 1) * gw]

        xw = (xs * decay_out_dt[:, g * gw:(g + 1) * gw]).astype(BF16)
        state_ref[g] = (state * decay_chunk[:, g * gw:(g + 1) * gw]
                        + lax.dot_general(b16, xw, (((0,), (0,)), ((), ())),
                                          preferred_element_type=F32))

        y = y * _silu(z_ref[0, :, g * gw:(g + 1) * gw].astype(F32))
        ms2 = jnp.mean(y * y, axis=-1, keepdims=True)
        y_ref[0, :, g * gw:(g + 1) * gw] = (
            y * lax.rsqrt(ms2 + EPS) * ng_ref[:, g * gw:(g + 1) * gw]).astype(BF16)
        if g in conv_after_group:
            conv_block(conv_after_group[g])

    win_ref[0:M_WIN_PAD, :] = win_ref[q:q + M_WIN_PAD, :]


def _ssd(main, dt_raw, conv_w, conv_b, dt_bias, a_log, d_skip, norm_g):
    b, s, _ = main.shape
    nc = s // M_CHUNK
    pad = LANES - M_HEADS
    head = lax.broadcasted_iota(jnp.int32, (LANES, M_D_INNER), 0)
    chan = lax.broadcasted_iota(jnp.int32, (LANES, M_D_INNER), 1)
    expand = (chan // M_HEAD_DIM == head).astype(BF16)
    win_rows = M_WIN_PAD + M_CHUNK
    out_row = lax.broadcasted_iota(jnp.int32, ((M_CONV - 1) * M_CHUNK, win_rows), 0)
    win_row = lax.broadcasted_iota(jnp.int32, ((M_CONV - 1) * M_CHUNK, win_rows), 1)
    shift = (win_row == out_row % M_CHUNK + out_row // M_CHUNK + M_WIN_PAD - (M_CONV - 1)).astype(BF16)
    vec = lambda n: pl.BlockSpec((1, n), lambda bi, c: (0, 0))
    blk = lambda j: pl.BlockSpec((1, M_CHUNK, M_D_INNER), lambda bi, c: (bi, c, j))
    return pl.pallas_call(
        _ssd_kernel,
        grid=(b, nc),
        in_specs=[blk(0), blk(1), blk(2),
                  pl.BlockSpec((1, M_CHUNK, LANES), lambda bi, c: (bi, c, 0)),
                  pl.BlockSpec((M_CONV, M_CONV_DIM), lambda bi, c: (0, 0)),
                  vec(M_CONV_DIM), vec(LANES), vec(LANES), vec(M_D_INNER), vec(M_D_INNER),
                  pl.BlockSpec((LANES, M_D_INNER), lambda bi, c: (0, 0)),
                  pl.BlockSpec(shift.shape, lambda bi, c: (0, 0))],
        out_specs=pl.BlockSpec((1, M_CHUNK, M_D_INNER), lambda bi, c: (bi, c, 0)),
        out_shape=jax.ShapeDtypeStruct((b, s, M_D_INNER), BF16),
        scratch_shapes=[pltpu.VMEM((win_rows, M_CONV_DIM), BF16),
                        pltpu.VMEM((M_CHUNK, M_CONV_DIM), F32),
                        pltpu.VMEM((M_GROUPS, M_STATE, M_GROUP_W), F32)],
        compiler_params=_cparams(("parallel", "arbitrary")),
        name="ssd",
    )(main, main, main, dt_raw, conv_w, conv_b[None, :],
      jnp.pad(dt_bias, (0, pad))[None, :], jnp.pad(a_log, (0, pad))[None, :],
      jnp.repeat(d_skip, M_HEAD_DIM)[None, :], norm_g[None, :], expand, shift)


def _swiglu_kernel(x_ref, sc_ref, sh_ref, g_ref, wg_ref, wu_ref, wd_ref, o_ref, *, f_chunk):
    x = x_ref[0]
    h = _norm_mod(x, sc_ref[0], sh_ref[0]).astype(BF16)
    f = wg_ref.shape[1]
    acc = jnp.zeros(x.shape, F32)
    for c0 in range(0, f, f_chunk):
        c1 = min(c0 + f_chunk, f)
        gate = jnp.dot(h, wg_ref[:, c0:c1], preferred_element_type=F32)
        up = jnp.dot(h, wu_ref[:, c0:c1], preferred_element_type=F32)
        acc = acc + jnp.dot((_silu(gate) * up).astype(BF16), wd_ref[c0:c1, :],
                            preferred_element_type=F32)
    o_ref[0] = x + g_ref[0] * acc


def _swiglu(x, sc, sh, g, wg, wu, wd, *, tm=512, f_chunk=512):
    b, s, d = x.shape
    f = wg.shape[1]
    tm = min(tm, s)
    mod = pl.BlockSpec((1, 1, d), lambda bi, i: (bi, 0, 0))
    return pl.pallas_call(
        functools.partial(_swiglu_kernel, f_chunk=f_chunk),
        grid=(b, s // tm),
        in_specs=[pl.BlockSpec((1, tm, d), lambda bi, i: (bi, i, 0)), mod, mod, mod,
                  _resident((d, f)), _resident((d, f)), _resident((f, d))],
        out_specs=pl.BlockSpec((1, tm, d), lambda bi, i: (bi, i, 0)),
        out_shape=jax.ShapeDtypeStruct((b, s, d), F32),
        compiler_params=_cparams(("parallel", "parallel")),
        name="swiglu",
    )(x, sc, sh, g, wg, wu, wd)


A_V_ROWS = A_V_DIM + 16


def _attn_kernel(q_ref, k_ref, v_ref, lam_ref, g_ref, o_ref, vt_ref, qm_ref, st_ref, p_ref, m_ref,
                 al_ref, acc_ref, *, tq, tk, lambda_init):
    qi = pl.program_id(2)

    @pl.when(qi == 0)
    def _():
        for j in range(vt_ref.shape[0]):
            vt_ref[j, 0:A_V_DIM, :] = v_ref[0, j * tk:(j + 1) * tk, :].astype(F32).T.astype(BF16)
            vt_ref[j, A_V_DIM:A_V_ROWS, :] = jnp.ones((A_V_ROWS - A_V_DIM, tk), BF16)

    lane = lax.broadcasted_iota(jnp.int32, (tq, LANES), 1)
    q = q_ref[0]
    q = (q.astype(F32) * (A_HEAD_DIM ** -0.5)).astype(BF16)
    zero = jnp.zeros_like(q)
    qm_ref[0] = jnp.where(lane < A_HEAD_DIM, q, zero)
    qm_ref[1] = jnp.where(lane >= A_HEAD_DIM, q, zero)
    m_ref[...] = jnp.full(m_ref.shape, -jnp.inf, F32)
    acc_ref[...] = jnp.zeros(acc_ref.shape, F32)
    p_ref[...] = jnp.zeros(p_ref.shape, BF16)
    al_ref[...] = jnp.ones(al_ref.shape, F32)

    last = (qi * tq) // tk

    def scores(blk, j):
        start = pl.multiple_of(blk * tk, tk)
        st_ref[j] = lax.dot_general(k_ref[0, pl.ds(start, tk), :], qm_ref[j],
                                    (((1,), (1,)), ((), ())), preferred_element_type=F32)

    def softmax(blk, j, masked):
        st = st_ref[j]
        if masked:
            key = blk * tk + lax.broadcasted_iota(jnp.int32, (tk, tq), 0)
            qry = qi * tq + lax.broadcasted_iota(jnp.int32, (tk, tq), 1)
            st = jnp.where(key <= qry, st, -jnp.inf)
        m_old = m_ref[j]
        top = st
        while top.shape[0] > 32:
            half = top.shape[0] // 2
            top = jnp.maximum(top[:half], top[half:])
        m_new = jnp.maximum(m_old, jnp.max(top, axis=0, keepdims=True))
        al_ref[j] = jnp.exp(m_old - m_new)
        p_ref[j] = jnp.exp(st - m_new).astype(BF16)
        m_ref[j] = m_new

    def weighted_values(blk, j):
        acc_ref[j] = al_ref[j] * acc_ref[j] + jnp.dot(vt_ref[blk], p_ref[j],
                                                      preferred_element_type=F32)

    for j in range(2):
        scores(0, j)

    def visit(blk, diagonal):
        for j in range(2):
            weighted_values(jnp.maximum(blk - 1, 0), j)
            softmax(blk, j, diagonal)
            if not diagonal:
                scores(blk + 1, j)

    def body(blk, carry):
        visit(blk, False)
        return carry

    lax.fori_loop(0, last, body, 0)
    visit(last, True)
    for j in range(2):
        weighted_values(last, j)

    lam_v = lam_ref[...]
    lam = (jnp.exp(jnp.sum(lam_v[0:1] * lam_v[1:2], axis=-1, keepdims=True))
           - jnp.exp(jnp.sum(lam_v[2:3] * lam_v[3:4], axis=-1, keepdims=True)) + lambda_init)
    acc0 = acc_ref[0]
    acc1 = acc_ref[1]
    o = (acc0[0:A_V_DIM] / acc0[A_V_DIM:A_V_DIM + 1]
         - lam * (acc1[0:A_V_DIM] / acc1[A_V_DIM:A_V_DIM + 1])).T
    o = o * lax.rsqrt(jnp.mean(o * o, axis=-1, keepdims=True) + EPS)
    o_ref[0] = (o * g_ref[...] * (1.0 - lambda_init)).astype(BF16)


def _attention(qkv, lam_pack, subln_g, lambda_init):
    b, s, _ = qkv.shape
    tq = min(A_TQ, s)
    tk = min(A_TK, s)
    return pl.pallas_call(
        functools.partial(_attn_kernel, tq=tq, tk=tk, lambda_init=lambda_init),
        grid=(b, A_HEADS, s // tq),
        in_specs=[pl.BlockSpec((1, tq, LANES), lambda bi, h, i: (bi, i, h)),
                  pl.BlockSpec((1, s, LANES), lambda bi, h, i: (bi, 0, A_HEADS + h)),
                  pl.BlockSpec((1, s, LANES), lambda bi, h, i: (bi, 0, 2 * A_HEADS + h)),
                  pl.BlockSpec((8, LANES), lambda bi, h, i: (0, 0)),
                  pl.BlockSpec((1, LANES), lambda bi, h, i: (0, 0))],
        out_specs=pl.BlockSpec((1, tq, LANES), lambda bi, h, i: (bi, i, h)),
        out_shape=jax.ShapeDtypeStruct((b, s, A_HEADS * A_V_DIM), BF16),
        scratch_shapes=[pltpu.VMEM((s // tk, A_V_ROWS, tk), BF16),
                        pltpu.VMEM((2, tq, LANES), BF16),
                        pltpu.VMEM((2, tk, tq), F32),
                        pltpu.VMEM((2, tk, tq), BF16),
                        pltpu.VMEM((2, 1, tq), F32),
                        pltpu.VMEM((2, 1, tq), F32),
                        pltpu.VMEM((2, A_V_ROWS, tq), F32)],
        compiler_params=_cparams(("parallel", "parallel", "arbitrary")),
        name="diff_attn",
    )(qkv, qkv, qkv, lam_pack, subln_g[None, :])


def _router_kernel(x_ref, sc_ref, sh_ref, wr_ref, h_ref, info_ref, cnt_ref, run_ref):
    tm = x_ref.shape[1]

    @pl.when((pl.program_id(0) == 0) & (pl.program_id(1) == 0))
    def _():
        run_ref[...] = jnp.zeros_like(run_ref)

    h = _norm_mod(x_ref[0], sc_ref[0], sh_ref[0])
    h_ref[...] = h
    logits = jnp.dot(h, wr_ref[...], preferred_element_type=F32, precision=HIGHEST)
    lane = lax.broadcasted_iota(jnp.int32, (tm, LANES), 1)
    lg = jnp.where(lane < N_EXPERTS, logits, -jnp.inf)
    m0 = jnp.max(lg, axis=-1, keepdims=True)
    e0 = jnp.min(jnp.where(lg == m0, lane, LANES), axis=-1, keepdims=True)
    lg1 = jnp.where(lane == e0, -jnp.inf, lg)
    m1 = jnp.max(lg1, axis=-1, keepdims=True)
    e1 = jnp.min(jnp.where(lg1 == m1, lane, LANES), axis=-1, keepdims=True)
    ex = jnp.exp(m1 - m0)
    g0 = 1.0 / (1.0 + ex)
    g1 = ex / (1.0 + ex)

    pick0 = lane == e0
    pick1 = lane == e1
    onehot = (pick0 | pick1).astype(BF16)
    ri = lax.broadcasted_iota(jnp.int32, (tm, tm), 0)
    ci = lax.broadcasted_iota(jnp.int32, (tm, tm), 1)
    before = jnp.dot((ci < ri).astype(BF16), onehot, preferred_element_type=F32) + run_ref[0:1, :]
    rank0 = jnp.sum(jnp.where(pick0, before, 0.0), axis=-1, keepdims=True)
    rank1 = jnp.sum(jnp.where(pick1, before, 0.0), axis=-1, keepdims=True)
    total = run_ref[0:1, :] + jnp.sum(onehot.astype(F32), axis=0, keepdims=True)
    run_ref[...] = jnp.broadcast_to(total, run_ref.shape)
    cnt_ref[...] = jnp.broadcast_to(total, cnt_ref.shape)

    info = jnp.zeros((tm, LANES), F32)
    for i, v in enumerate((e0.astype(F32), e1.astype(F32), g0, g1, rank0, rank1)):
        info = jnp.where(lane == i, v, info)
    info_ref[...] = info


def _router(x, sc, sh, w_router, *, tm=256):
    b, s, d = x.shape
    tm = min(tm, s)
    nt = s // tm
    mod = pl.BlockSpec((1, 1, d), lambda bi, i: (bi, 0, 0))
    return pl.pallas_call(
        _router_kernel,
        grid=(b, nt),
        in_specs=[pl.BlockSpec((1, tm, d), lambda bi, i: (bi, i, 0)), mod, mod,
                  pl.BlockSpec((d, LANES), lambda bi, i: (0, 0))],
        out_specs=[pl.BlockSpec((tm, d), lambda bi, i: (bi * nt + i, 0)),
                   pl.BlockSpec((tm, LANES), lambda bi, i: (bi * nt + i, 0)),
                   pl.BlockSpec((8, LANES), lambda bi, i: (0, 0))],
        out_shape=[jax.ShapeDtypeStruct((b * s, d), F32),
                   jax.ShapeDtypeStruct((b * s, LANES), F32),
                   jax.ShapeDtypeStruct((8, LANES), F32)],
        scratch_shapes=[pltpu.VMEM((8, LANES), F32)],
        compiler_params=_cparams(("arbitrary", "arbitrary")),
        name="moe_router",
    )(x, sc, sh, jnp.pad(w_router, ((0, 0), (0, LANES - N_EXPERTS))))


def _row_copy(src_ref, src_row, dst_ref, dst_row, sem):
    return pltpu.make_async_copy(src_ref.at[pl.ds(src_row, 1)], dst_ref.at[pl.ds(dst_row, 1)], sem)


def _dispatch_kernel(dest_ref, h_ref, init_ref, xs_ref, sem):
    del init_ref
    tm = h_ref.shape[0]

    def issue(i, carry):
        for k in range(TOP_K):
            _row_copy(h_ref, i, xs_ref, dest_ref[TOP_K * i + k], sem).start(priority=k)
        return carry

    def drain(i, carry):
        for k in range(TOP_K):
            _row_copy(h_ref, 0, xs_ref, 0, sem).wait()
        return carry

    lax.fori_loop(0, tm, issue, 0, unroll=8)
    lax.fori_loop(0, tm, drain, 0, unroll=8)


def _dispatch(h, dest, n_rows, *, tm=256):
    t, d = h.shape
    tm = min(tm, t)
    return pl.pallas_call(
        _dispatch_kernel,
        grid=(t // tm,),
        in_specs=[pl.BlockSpec((TOP_K * tm,), lambda i: (i,), memory_space=pltpu.SMEM),
                  pl.BlockSpec((tm, d), lambda i: (i, 0)),
                  pl.BlockSpec(memory_space=pl.ANY)],
        out_specs=pl.BlockSpec(memory_space=pl.ANY),
        out_shape=jax.ShapeDtypeStruct((n_rows, d), F32),
        scratch_shapes=[pltpu.SemaphoreType.DMA],
        input_output_aliases={2: 0},
        compiler_params=_cparams(("arbitrary",)),
        name="moe_dispatch",
    )(dest, h, jnp.zeros((n_rows, d), F32))


def _expert_kernel(be_ref, bv_ref, xs_ref, wg_ref, wu_ref, wd_ref, y_ref, *, f_chunk):
    del be_ref
    i = pl.program_id(0)

    @pl.when(bv_ref[i] != 0)
    def _():
        x = xs_ref[...].astype(BF16)
        f = wg_ref.shape[2]
        acc = jnp.zeros(y_ref.shape, F32)
        for c0 in range(0, f, f_chunk):
            gate = jnp.dot(x, wg_ref[0, :, c0:c0 + f_chunk], preferred_element_type=F32)
            up = jnp.dot(x, wu_ref[0, :, c0:c0 + f_chunk], preferred_element_type=F32)
            acc = acc + jnp.dot((_silu(gate) * up).astype(BF16), wd_ref[0, c0:c0 + f_chunk, :],
                                preferred_element_type=F32)
        y_ref[...] = acc

    @pl.when(bv_ref[i] == 0)
    def _():
        y_ref[...] = jnp.zeros_like(y_ref)


def _experts(xs, block_exp, block_valid, wg, wu, wd, *, f_chunk=512):
    n_rows, d = xs.shape
    f = wg.shape[2]
    n_blocks = n_rows // MOE_ROWS
    wspec = lambda shape: pl.BlockSpec(shape, lambda i, be, bv: (be[i], 0, 0),
                                       pipeline_mode=pl.Buffered(1))
    return pl.pallas_call(
        functools.partial(_expert_kernel, f_chunk=f_chunk),
        grid_spec=pltpu.PrefetchScalarGridSpec(
            num_scalar_prefetch=2,
            grid=(n_blocks,),
            in_specs=[pl.BlockSpec((MOE_ROWS, d), lambda i, be, bv: (i, 0)),
                      wspec((1, d, f)), wspec((1, d, f)), wspec((1, f, d))],
            out_specs=pl.BlockSpec((MOE_ROWS, d), lambda i, be, bv: (i, 0))),
        out_shape=jax.ShapeDtypeStruct((n_rows, d), F32),
        compiler_params=_cparams(("arbitrary",)),
        name="moe_experts",
    )(block_exp, block_valid, xs, wg, wu, wd)


def _combine_kernel(dest_ref, next_dest_ref, info_ref, x_ref, g_ref, fg_ref, y_ref, o_ref, buf_ref,
                    sem):
    tm = x_ref.shape[0]
    step = pl.program_id(0)
    slot = step % 2

    def gather(idx_ref, to_slot):
        def issue(i, carry):
            for k in range(TOP_K):
                _row_copy(y_ref, idx_ref[TOP_K * i + k], buf_ref.at[to_slot, k], i,
                          sem.at[to_slot]).start(priority=k)
            return carry
        lax.fori_loop(0, tm, issue, 0, unroll=8)

    @pl.when(step == 0)
    def _():
        gather(dest_ref, 0)

    @pl.when(step + 1 < pl.num_programs(0))
    def _():
        gather(next_dest_ref, 1 - slot)

    def drain(i, carry):
        for k in range(TOP_K):
            _row_copy(y_ref, 0, buf_ref.at[slot, k], 0, sem.at[slot]).wait()
        return carry

    lax.fori_loop(0, tm, drain, 0, unroll=8)
    info = info_ref[...]
    ffn = info[:, 2:3] * buf_ref[slot, 0] + info[:, 3:4] * buf_ref[slot, 1]
    x = x_ref[...] + g_ref[0] * ffn
    o_ref[...] = x * lax.rsqrt(jnp.mean(x * x, axis=-1, keepdims=True) + EPS) * fg_ref[...]


def _combine(dest, info, x2d, g, final_g, y, seq, *, tm=256):
    t, d = x2d.shape
    tm = min(tm, seq)
    per_seq = seq // tm
    n_tiles = t // tm
    return pl.pallas_call(
        _combine_kernel,
        grid=(n_tiles,),
        in_specs=[pl.BlockSpec((TOP_K * tm,), lambda i: (i,), memory_space=pltpu.SMEM),
                  pl.BlockSpec((TOP_K * tm,), lambda i: (jnp.minimum(i + 1, n_tiles - 1),),
                               memory_space=pltpu.SMEM),
                  pl.BlockSpec((tm, LANES), lambda i: (i, 0)),
                  pl.BlockSpec((tm, d), lambda i: (i, 0)),
                  pl.BlockSpec((1, 1, d), lambda i: (i // per_seq, 0, 0)),
                  pl.BlockSpec((1, d), lambda i: (0, 0)),
                  pl.BlockSpec(memory_space=pl.ANY)],
        out_specs=pl.BlockSpec((tm, d), lambda i: (i, 0)),
        out_shape=jax.ShapeDtypeStruct((t, d), F32),
        scratch_shapes=[pltpu.VMEM((2, TOP_K, tm, d), F32), pltpu.SemaphoreType.DMA((2,))],
        compiler_params=_cparams(("arbitrary",)),
        name="moe_combine",
    )(dest, dest, info, x2d, g, final_g[None, :], y)


def _moe_plan(info, counts):
    e = info[:, 0:TOP_K].astype(jnp.int32)
    rank = info[:, 4:4 + TOP_K].astype(jnp.int32)
    cnt = counts[0, :N_EXPERTS].astype(jnp.int32)
    padded = (cnt + MOE_ROWS - 1) // MOE_ROWS * MOE_ROWS
    pends = jnp.cumsum(padded)
    pstarts = pends - padded
    start_of = jnp.zeros_like(e)
    for j in range(N_EXPERTS):
        start_of = jnp.where(e == j, pstarts[j], start_of)
    dest = (start_of + rank).reshape(-1)
    n_blocks = (TOP_K * info.shape[0]) // MOE_ROWS + N_EXPERTS
    block_start = jnp.arange(n_blocks, dtype=jnp.int32) * MOE_ROWS
    valid = block_start < pends[-1]
    last_start = jnp.maximum(pends[-1] - MOE_ROWS, 0)
    block_exp = jnp.searchsorted(pends, jnp.where(valid, block_start, last_start), side='right')
    block_exp = jnp.minimum(block_exp, N_EXPERTS - 1).astype(jnp.int32)
    return dest, block_exp, valid.astype(jnp.int32), n_blocks * MOE_ROWS


def kernel(x, c, ada_w0, ada_b0, m_w_in, m_conv_w, m_conv_b, m_dt_bias, m_a_log, m_d_skip, m_norm_g, m_w_out, ffn_w_gate, ffn_w_up, ffn_w_down, ada_w1, ada_b1, a_w_qkv, a_lam_q1, a_lam_k1, a_lam_q2, a_lam_k2, a_subln_g, a_w_o, moe_w_router, moe_w_gate, moe_w_up, moe_w_down, final_g):
    b, s, d = x.shape
    bf = lambda w: w.astype(BF16)
    c8 = jnp.pad(c, ((0, 8 - b), (0, 0)))

    def mods(w, bias):
        mod = _adaln(c8, w, bias[None, :])[:b]
        return [m[:, None, :] for m in jnp.split(mod, 6, axis=-1)]

    sh1, sc1, g1, sh2, sc2, g2 = mods(ada_w0, ada_b0)
    w_dt = jnp.pad(m_w_in[:, M_MAIN_W:], ((0, 0), (0, LANES - M_HEADS)))
    main, dt_raw = _norm_proj(x, sc1, sh1, bf(m_w_in[:, :M_MAIN_W]), bf(w_dt))
    y = _ssd(main, dt_raw, m_conv_w, m_conv_b, m_dt_bias, m_a_log, m_d_skip, m_norm_g)
    x = _proj_res(y, bf(m_w_out), x, g1)
    x = _swiglu(x, sc2, sh2, g2, bf(ffn_w_gate), bf(ffn_w_up), bf(ffn_w_down))

    sh1, sc1, g1, sh2, sc2, g2 = mods(ada_w1, ada_b1)
    lambda_init = 0.8 - 0.6 * math.exp(-0.3 * 1)
    qkv = _norm_proj(x, sc1, sh1, bf(a_w_qkv))
    lam_pack = jnp.pad(jnp.stack([a_lam_q1, a_lam_k1, a_lam_q2, a_lam_k2]),
                       ((0, 4), (0, LANES - A_HEAD_DIM)))
    att = _attention(qkv, lam_pack, a_subln_g, lambda_init)
    x = _proj_res(att, bf(a_w_o), x, g1)

    h, info, counts = _router(x, sc2, sh2, moe_w_router)
    dest, block_exp, block_valid, n_rows = _moe_plan(info, counts)
    xs = _dispatch(h, dest, n_rows)
    ye = _experts(xs, block_exp, block_valid, bf(moe_w_gate), bf(moe_w_up), bf(moe_w_down))
    out = _combine(dest, info, x.reshape(b * s, d), g2, final_g, ye, s)
    return out.reshape(b, s, d)
```

```python
import functools
import math

import jax
import jax.numpy as jnp
from jax import lax
from jax.experimental import pallas as pl
from jax.experimental.pallas import tpu as pltpu

F32 = jnp.float32
BF16 = jnp.bfloat16
HIGHEST = lax.Precision.HIGHEST

D_MODEL = 1024
EPS = 1e-6
M_D_INNER = 2048
M_HEAD_DIM = 64
M_HEADS = 32
M_GROUPS = 8
M_HPG = 4
M_STATE = 128
M_CONV = 4
M_CHUNK = 128
M_CONV_COLS = 512
M_WIN_PAD = 16
M_GROUP_W = M_HPG * M_HEAD_DIM
M_BC_W = 2 * M_GROUPS * M_STATE
M_CONV_DIM = M_D_INNER + M_BC_W
M_MAIN_W = M_D_INNER + M_CONV_DIM
A_HEADS = 8
A_HEAD_DIM = 64
A_V_DIM = 128
A_TQ = 512
A_TK = 512
FFN_DIM = 2816
N_EXPERTS = 8
TOP_K = 2
EXPERT_DIM = 3584
MOE_ROWS = 512
LANES = 128
VMEM_LIMIT = 56 * 1024 * 1024


def _silu(v):
    return v * jax.nn.sigmoid(v)


def _norm_mod(x, sc, sh):
    ms = jnp.mean(x * x, axis=-1, keepdims=True)
    return x * lax.rsqrt(ms + EPS) * (1.0 + sc) + sh


def _cparams(sem):
    return pltpu.CompilerParams(dimension_semantics=sem, vmem_limit_bytes=VMEM_LIMIT)


def _resident(shape):
    nd = len(shape)
    return pl.BlockSpec(shape, lambda *_: (0,) * nd, pipeline_mode=pl.Buffered(1))


def _adaln_kernel(c_ref, w_ref, b_ref, o_ref):
    o_ref[...] = jnp.dot(_silu(c_ref[...]), w_ref[...], preferred_element_type=F32,
                         precision=HIGHEST) + b_ref[...]


def _adaln(c8, w, b):
    rows, d = c8.shape
    n = w.shape[1]
    tn = 1024
    return pl.pallas_call(
        _adaln_kernel,
        grid=(n // tn,),
        in_specs=[pl.BlockSpec((rows, d), lambda j: (0, 0)),
                  pl.BlockSpec((d, tn), lambda j: (0, j)),
                  pl.BlockSpec((1, tn), lambda j: (0, j))],
        out_specs=pl.BlockSpec((rows, tn), lambda j: (0, j)),
        out_shape=jax.ShapeDtypeStruct((rows, n), F32),
        compiler_params=_cparams(("arbitrary",)),
        name="adaln",
    )(c8, w, b)


def _norm_proj_kernel(x_ref, sc_ref, sh_ref, w_ref, *rest, n_chunk, has_aux):
    if has_aux:
        waux_ref, o_ref, aux_ref = rest
    else:
        (o_ref,) = rest
    h = _norm_mod(x_ref[0], sc_ref[0], sh_ref[0]).astype(BF16)
    n = w_ref.shape[1]
    for c0 in range(0, n, n_chunk):
        res = jnp.dot(h, w_ref[:, c0:c0 + n_chunk],
                      preferred_element_type=F32).astype(o_ref.dtype)
        if len(o_ref.shape) == 3:
            o_ref[0, :, c0:c0 + n_chunk] = res
        else:
            for l0 in range(0, n_chunk, LANES):
                o_ref[0, (c0 + l0) // LANES] = res[:, l0:l0 + LANES]
    if has_aux:
        aux_ref[0] = jnp.dot(h, waux_ref[...], preferred_element_type=F32)


def _norm_proj(x, sc, sh, w, w_aux=None, *, head_major=False, tm=512, n_chunk=1024):
    b, s, d = x.shape
    n = w.shape[1]
    tm = min(tm, s)
    in_specs = [pl.BlockSpec((1, tm, d), lambda bi, i: (bi, i, 0)),
                pl.BlockSpec((1, 1, d), lambda bi, i: (bi, 0, 0)),
                pl.BlockSpec((1, 1, d), lambda bi, i: (bi, 0, 0)),
                _resident((d, n))]
    if head_major:
        out_specs = [pl.BlockSpec((1, n // LANES, tm, LANES), lambda bi, i: (bi, 0, i, 0))]
        out_shape = [jax.ShapeDtypeStruct((b, n // LANES, s, LANES), BF16)]
    else:
        out_specs = [pl.BlockSpec((1, tm, n), lambda bi, i: (bi, i, 0))]
        out_shape = [jax.ShapeDtypeStruct((b, s, n), BF16)]
    args = [x, sc, sh, w]
    if w_aux is not None:
        na = w_aux.shape[1]
        in_specs.append(_resident((d, na)))
        out_specs.append(pl.BlockSpec((1, tm, na), lambda bi, i: (bi, i, 0)))
        out_shape.append(jax.ShapeDtypeStruct((b, s, na), F32))
        args.append(w_aux)
    out = pl.pallas_call(
        functools.partial(_norm_proj_kernel, n_chunk=n_chunk, has_aux=w_aux is not None),
        grid=(b, s // tm),
        in_specs=in_specs, out_specs=out_specs, out_shape=out_shape,
        compiler_params=_cparams(("parallel", "parallel")),
        name="norm_proj",
    )(*args)
    return out if w_aux is not None else out[0]


def _proj_res_kernel(a_ref, w_ref, x_ref, g_ref, o_ref):
    if len(a_ref.shape) == 3:
        a = a_ref[0]
    else:
        a = jnp.concatenate([a_ref[0, h] for h in range(a_ref.shape[1])], axis=1)
    o_ref[0] = x_ref[0] + g_ref[0] * jnp.dot(a, w_ref[...], preferred_element_type=F32)


def _proj_res(a, w, x, g, *, tm=512):
    b, s, _ = x.shape
    k, d = w.shape
    tm = min(tm, s)
    if len(a.shape) == 3:
        a_spec = pl.BlockSpec((1, tm, k), lambda bi, i: (bi, i, 0))
    else:
        a_spec = pl.BlockSpec((1, k // LANES, tm, LANES), lambda bi, i: (bi, 0, i, 0))
    return pl.pallas_call(
        _proj_res_kernel,
        grid=(b, s // tm),
        in_specs=[a_spec,
                  _resident((k, d)),
                  pl.BlockSpec((1, tm, d), lambda bi, i: (bi, i, 0)),
                  pl.BlockSpec((1, 1, d), lambda bi, i: (bi, 0, 0))],
        out_specs=pl.BlockSpec((1, tm, d), lambda bi, i: (bi, i, 0)),
        out_shape=jax.ShapeDtypeStruct((b, s, d), F32),
        compiler_params=_cparams(("parallel", "parallel")),
        name="proj_res",
    )(a, w, x, g)


def _softplus(v):
    return jnp.maximum(v, 0.0) + jnp.log1p(jnp.exp(-jnp.abs(v)))


def _ssd_kernel(z_ref, x_ref, bc_ref, dt_ref, convw_ref, convb_ref, dtb_ref, alog_ref, dskip_ref,
                ng_ref, expand_ref, shift_ref, y_ref, win_ref, xbc_ref, state_ref):
    q = M_CHUNK
    gw = M_GROUP_W

    @pl.when(pl.program_id(1) == 0)
    def _():
        win_ref[0:M_WIN_PAD, :] = jnp.zeros((M_WIN_PAD, M_CONV_DIM), BF16)
        state_ref[...] = jnp.zeros_like(state_ref)

    win_ref[M_WIN_PAD:M_WIN_PAD + q, 0:M_D_INNER] = x_ref[0]
    win_ref[M_WIN_PAD:M_WIN_PAD + q, M_D_INNER:M_CONV_DIM] = bc_ref[0]
    shift = shift_ref[...]

    def conv_block(j):
        cols = slice(j * M_CONV_COLS, (j + 1) * M_CONV_COLS)
        shifted = jnp.dot(shift, win_ref[:, cols], preferred_element_type=F32)
        acc = (convb_ref[:, cols] + convw_ref[M_CONV - 1:M_CONV, cols]
               * win_ref[M_WIN_PAD:M_WIN_PAD + q, cols].astype(F32))
        for k in range(M_CONV - 1):
            acc = acc + convw_ref[k:k + 1, cols] * shifted[k * q:(k + 1) * q, :]
        xbc_ref[:, cols] = _silu(acc)

    for j in (4, 6, 0):
        conv_block(j)
    conv_after_group = {0: 1, 1: 5, 2: 7, 3: 2, 4: 3}

    dt = _softplus(dt_ref[0] + dtb_ref[...])
    da = dt * (-jnp.exp(alog_ref[...]))
    ri = lax.broadcasted_iota(jnp.int32, (q, q), 0)
    ci = lax.broadcasted_iota(jnp.int32, (q, q), 1)
    tril = ci <= ri
    acs = jnp.dot(tril.astype(F32), da, preferred_element_type=F32, precision=HIGHEST)
    acs_t = acs.T
    dt_t = dt.T
    a_last = acs[q - 1:q, :]

    expand = expand_ref[...]

    def per_channel(v):
        hi = v.astype(BF16)
        lo = (v - hi.astype(F32)).astype(BF16)
        return (jnp.dot(hi, expand, preferred_element_type=F32)
                + jnp.dot(lo, expand, preferred_element_type=F32))

    decay_in = per_channel(jnp.exp(acs))
    decay_out_dt = per_channel(jnp.exp(a_last - acs) * dt)
    decay_chunk = decay_in[q - 1:q, :]

    lane = lax.broadcasted_iota(jnp.int32, (q, LANES), 1)
    for g in range(M_GROUPS):
        b16 = xbc_ref[:, M_D_INNER + g * M_STATE:M_D_INNER + (g + 1) * M_STATE].astype(BF16)
        c16 = xbc_ref[:, M_D_INNER + (M_GROUPS + g) * M_STATE:
                  M_D_INNER + (M_GROUPS + g + 1) * M_STATE].astype(BF16)
        xs = xbc_ref[:, g * gw:(g + 1) * gw]
        xs16 = xs.astype(BF16)
        cb = lax.dot_general(c16, b16, (((1,), (1,)), ((), ())), preferred_element_type=F32)
        state = state_ref[g]
        y_off = (jnp.dot(c16, state.astype(BF16), preferred_element_type=F32)
                 * decay_in[:, g * gw:(g + 1) * gw])
        pairs = []
        for p in range(M_HPG // 2):
            ms = []
            for r in range(2):
                h = g * M_HPG + 2 * p + r
                seg = acs[:, h:h + 1] - acs_t[h:h + 1, :]
                decay = jnp.exp(jnp.where(tril, seg, -jnp.inf))
                ms.append((cb * decay * dt_t[h:h + 1, :]).astype(BF16))
            x2 = xs16[:, p * LANES:(p + 1) * LANES]
            zero = jnp.zeros_like(x2)
            rhs = jnp.concatenate([jnp.where(lane < M_HEAD_DIM, x2, zero),
                                   jnp.where(lane >= M_HEAD_DIM, x2, zero)], axis=0)
            pairs.append(jnp.dot(jnp.concatenate(ms, axis=1), rhs, preferred_element_type=F32))
        y = jnp.concatenate(pairs, axis=1) + y_off + xs * dskip_ref[:, g * gw:(g + 1) * gw]

        xw = (xs * decay_out_dt[:, g * gw:(g + 1) * gw]).astype(BF16)
        state_ref[g] = (state * decay_chunk[:, g * gw:(g + 1) * gw]
                        + lax.dot_general(b16, xw, (((0,), (0,)), ((), ())),
                                          preferred_element_type=F32))

        y = y * _silu(z_ref[0, :, g * gw:(g + 1) * gw].astype(F32))
        ms2 = jnp.mean(y * y, axis=-1, keepdims=True)
        y_ref[0, :, g * gw:(g + 1) * gw] = (
            y * lax.rsqrt(ms2 + EPS) * ng_ref[:, g * gw:(g + 1) * gw]).astype(BF16)
        if g in conv_after_group:
            conv_block(conv_after_group[g])

    win_ref[0:M_WIN_PAD, :] = win_ref[q:q + M_WIN_PAD, :]


def _ssd(main, dt_raw, conv_w, conv_b, dt_bias, a_log, d_skip, norm_g):
    b, s, _ = main.shape
    nc = s // M_CHUNK
    pad = LANES - M_HEADS
    head = lax.broadcasted_iota(jnp.int32, (LANES, M_D_INNER), 0)
    chan = lax.broadcasted_iota(jnp.int32, (LANES, M_D_INNER), 1)
    expand = (chan // M_HEAD_DIM == head).astype(BF16)
    win_rows = M_WIN_PAD + M_CHUNK
    out_row = lax.broadcasted_iota(jnp.int32, ((M_CONV - 1) * M_CHUNK, win_rows), 0)
    win_row = lax.broadcasted_iota(jnp.int32, ((M_CONV - 1) * M_CHUNK, win_rows), 1)
    shift = (win_row == out_row % M_CHUNK + out_row // M_CHUNK + M_WIN_PAD - (M_CONV - 1)).astype(BF16)
    vec = lambda n: pl.BlockSpec((1, n), lambda bi, c: (0, 0))
    blk = lambda j: pl.BlockSpec((1, M_CHUNK, M_D_INNER), lambda bi, c: (bi, c, j))
    return pl.pallas_call(
        _ssd_kernel,
        grid=(b, nc),
        in_specs=[blk(0), blk(1), blk(2),
                  pl.BlockSpec((1, M_CHUNK, LANES), lambda bi, c: (bi, c, 0)),
                  pl.BlockSpec((M_CONV, M_CONV_DIM), lambda bi, c: (0, 0)),
                  vec(M_CONV_DIM), vec(LANES), vec(LANES), vec(M_D_INNER), vec(M_D_INNER),
                  pl.BlockSpec((LANES, M_D_INNER), lambda bi, c: (0, 0)),
                  pl.BlockSpec(shift.shape, lambda bi, c: (0, 0))],
        out_specs=pl.BlockSpec((1, M_CHUNK, M_D_INNER), lambda bi, c: (bi, c, 0)),
        out_shape=jax.ShapeDtypeStruct((b, s, M_D_INNER), BF16),
        scratch_shapes=[pltpu.VMEM((win_rows, M_CONV_DIM), BF16),
                        pltpu.VMEM((M_CHUNK, M_CONV_DIM), F32),
                        pltpu.VMEM((M_GROUPS, M_STATE, M_GROUP_W), F32)],
        compiler_params=_cparams(("parallel", "arbitrary")),
        name="ssd",
    )(main, main, main, dt_raw, conv_w, conv_b[None, :],
      jnp.pad(dt_bias, (0, pad))[None, :], jnp.pad(a_log, (0, pad))[None, :],
      jnp.repeat(d_skip, M_HEAD_DIM)[None, :], norm_g[None, :], expand, shift)


def _swiglu_kernel(x_ref, sc_ref, sh_ref, g_ref, wg_ref, wu_ref, wd_ref, o_ref, *, f_chunk):
    x = x_ref[0]
    h = _norm_mod(x, sc_ref[0], sh_ref[0]).astype(BF16)
    f = wg_ref.shape[1]
    acc = jnp.zeros(x.shape, F32)
    for c0 in range(0, f, f_chunk):
        c1 = min(c0 + f_chunk, f)
        gate = jnp.dot(h, wg_ref[:, c0:c1], preferred_element_type=F32)
        up = jnp.dot(h, wu_ref[:, c0:c1], preferred_element_type=F32)
        acc = acc + jnp.dot((_silu(gate) * up).astype(BF16), wd_ref[c0:c1, :],
                            preferred_element_type=F32)
    o_ref[0] = x + g_ref[0] * acc


def _swiglu(x, sc, sh, g, wg, wu, wd, *, tm=512, f_chunk=512):
    b, s, d = x.shape
    f = wg.shape[1]
    tm = min(tm, s)
    mod = pl.BlockSpec((1, 1, d), lambda bi, i: (bi, 0, 0))
    return pl.pallas_call(
        functools.partial(_swiglu_kernel, f_chunk=f_chunk),
        grid=(b, s // tm),
        in_specs=[pl.BlockSpec((1, tm, d), lambda bi, i: (bi, i, 0)), mod, mod, mod,
                  _resident((d, f)), _resident((d, f)), _resident((f, d))],
        out_specs=pl.BlockSpec((1, tm, d), lambda bi, i: (bi, i, 0)),
        out_shape=jax.ShapeDtypeStruct((b, s, d), F32),
        compiler_params=_cparams(("parallel", "parallel")),
        name="swiglu",
    )(x, sc, sh, g, wg, wu, wd)


A_V_ROWS = A_V_DIM + 16


def _attn_kernel(q_ref, k_ref, v_ref, lam_ref, g_ref, o_ref, vt_ref, qm_ref, st_ref, p_ref, m_ref,
                 al_ref, acc_ref, *, tq, tk, lambda_init):
    qi = pl.program_id(2)

    @pl.when(qi == 0)
    def _():
        for j in range(vt_ref.shape[0]):
            vt_ref[j, 0:A_V_DIM, :] = v_ref[0, 0, j * tk:(j + 1) * tk, :].astype(F32).T.astype(BF16)
            vt_ref[j, A_V_DIM:A_V_ROWS, :] = jnp.ones((A_V_ROWS - A_V_DIM, tk), BF16)

    lane = lax.broadcasted_iota(jnp.int32, (tq, LANES), 1)
    q = q_ref[0, 0]
    q = (q.astype(F32) * (A_HEAD_DIM ** -0.5)).astype(BF16)
    zero = jnp.zeros_like(q)
    qm_ref[0] = jnp.where(lane < A_HEAD_DIM, q, zero)
    qm_ref[1] = jnp.where(lane >= A_HEAD_DIM, q, zero)
    m_ref[...] = jnp.full(m_ref.shape, -jnp.inf, F32)
    acc_ref[...] = jnp.zeros(acc_ref.shape, F32)
    p_ref[...] = jnp.zeros(p_ref.shape, BF16)
    al_ref[...] = jnp.ones(al_ref.shape, F32)

    last = (qi * tq) // tk

    def scores(blk, j):
        start = pl.multiple_of(blk * tk, tk)
        st_ref[j] = lax.dot_general(k_ref[0, 0, pl.ds(start, tk), :], qm_ref[j],
                                    (((1,), (1,)), ((), ())), preferred_element_type=F32)

    def softmax(blk, j, masked):
        st = st_ref[j]
        if masked:
            key = blk * tk + lax.broadcasted_iota(jnp.int32, (tk, tq), 0)
            qry = qi * tq + lax.broadcasted_iota(jnp.int32, (tk, tq), 1)
            st = jnp.where(key <= qry, st, -jnp.inf)
        m_old = m_ref[j]
        top = st
        while top.shape[0] > 32:
            half = top.shape[0] // 2
            top = jnp.maximum(top[:half], top[half:])
        m_new = jnp.maximum(m_old, jnp.max(top, axis=0, keepdims=True))
        al_ref[j] = jnp.exp(m_old - m_new)
        p_ref[j] = jnp.exp(st - m_new).astype(BF16)
        m_ref[j] = m_new

    def weighted_values(blk, j):
        acc_ref[j] = al_ref[j] * acc_ref[j] + jnp.dot(vt_ref[blk], p_ref[j],
                                                      preferred_element_type=F32)

    for j in range(2):
        scores(0, j)

    def visit(blk, diagonal):
        for j in range(2):
            weighted_values(jnp.maximum(blk - 1, 0), j)
            softmax(blk, j, diagonal)
            if not diagonal:
                scores(blk + 1, j)

    def body(pair, carry):
        visit(2 * pair, False)
        visit(2 * pair + 1, False)
        return carry

    lax.fori_loop(0, last // 2, body, 0)

    @pl.when(last % 2 == 1)
    def _():
        visit(last - 1, False)

    visit(last, True)
    for j in range(2):
        weighted_values(last, j)

    lam_v = lam_ref[...]
    lam = (jnp.exp(jnp.sum(lam_v[0:1] * lam_v[1:2], axis=-1, keepdims=True))
           - jnp.exp(jnp.sum(lam_v[2:3] * lam_v[3:4], axis=-1, keepdims=True)) + lambda_init)
    acc0 = acc_ref[0]
    acc1 = acc_ref[1]
    o = (acc0[0:A_V_DIM] / acc0[A_V_DIM:A_V_DIM + 1]
         - lam * (acc1[0:A_V_DIM] / acc1[A_V_DIM:A_V_DIM + 1])).T
    o = o * lax.rsqrt(jnp.mean(o * o, axis=-1, keepdims=True) + EPS)
    o_ref[0, 0] = (o * g_ref[...] * (1.0 - lambda_init)).astype(BF16)


def _attention(qkv, lam_pack, subln_g, lambda_init):
    b, _, s, _ = qkv.shape
    tq = min(A_TQ, s)
    tk = min(A_TK, s)
    return pl.pallas_call(
        functools.partial(_attn_kernel, tq=tq, tk=tk, lambda_init=lambda_init),
        grid=(b, A_HEADS, s // tq),
        in_specs=[pl.BlockSpec((1, 1, tq, LANES), lambda bi, h, i: (bi, h, i, 0)),
                  pl.BlockSpec((1, 1, s, LANES), lambda bi, h, i: (bi, A_HEADS + h, 0, 0)),
                  pl.BlockSpec((1, 1, s, LANES), lambda bi, h, i: (bi, 2 * A_HEADS + h, 0, 0)),
                  pl.BlockSpec((8, LANES), lambda bi, h, i: (0, 0)),
                  pl.BlockSpec((1, LANES), lambda bi, h, i: (0, 0))],
        out_specs=pl.BlockSpec((1, 1, tq, LANES), lambda bi, h, i: (bi, h, i, 0)),
        out_shape=jax.ShapeDtypeStruct((b, A_HEADS, s, A_V_DIM), BF16),
        scratch_shapes=[pltpu.VMEM((s // tk, A_V_ROWS, tk), BF16),
                        pltpu.VMEM((2, tq, LANES), BF16),
                        pltpu.VMEM((2, tk, tq), F32),
                        pltpu.VMEM((2, tk, tq), BF16),
                        pltpu.VMEM((2, 1, tq), F32),
                        pltpu.VMEM((2, 1, tq), F32),
                        pltpu.VMEM((2, A_V_ROWS, tq), F32)],
        compiler_params=_cparams(("parallel", "parallel", "arbitrary")),
        name="diff_attn",
    )(qkv, qkv, qkv, lam_pack, subln_g[None, :])


def _router_kernel(x_ref, sc_ref, sh_ref, wr_ref, h_ref, info_ref, cnt_ref, run_ref):
    tm = x_ref.shape[1]

    @pl.when((pl.program_id(0) == 0) & (pl.program_id(1) == 0))
    def _():
        run_ref[...] = jnp.zeros_like(run_ref)

    h = _norm_mod(x_ref[0], sc_ref[0], sh_ref[0])
    h_ref[...] = h
    logits = jnp.dot(h, wr_ref[...], preferred_element_type=F32, precision=HIGHEST)
    lane = lax.broadcasted_iota(jnp.int32, (tm, LANES), 1)
    lg = jnp.where(lane < N_EXPERTS, logits, -jnp.inf)
    m0 = jnp.max(lg, axis=-1, keepdims=True)
    e0 = jnp.min(jnp.where(lg == m0, lane, LANES), axis=-1, keepdims=True)
    lg1 = jnp.where(lane == e0, -jnp.inf, lg)
    m1 = jnp.max(lg1, axis=-1, keepdims=True)
    e1 = jnp.min(jnp.where(lg1 == m1, lane, LANES), axis=-1, keepdims=True)
    ex = jnp.exp(m1 - m0)
    g0 = 1.0 / (1.0 + ex)
    g1 = ex / (1.0 + ex)

    pick0 = lane == e0
    pick1 = lane == e1
    onehot = (pick0 | pick1).astype(BF16)
    ri = lax.broadcasted_iota(jnp.int32, (tm, tm), 0)
    ci = lax.broadcasted_iota(jnp.int32, (tm, tm), 1)
    before = jnp.dot((ci < ri).astype(BF16), onehot, preferred_element_type=F32) + run_ref[0:1, :]
    rank0 = jnp.sum(jnp.where(pick0, before, 0.0), axis=-1, keepdims=True)
    rank1 = jnp.sum(jnp.where(pick1, before, 0.0), axis=-1, keepdims=True)
    total = run_ref[0:1, :] + jnp.sum(onehot.astype(F32), axis=0, keepdims=True)
    run_ref[...] = jnp.broadcast_to(total, run_ref.shape)
    cnt_ref[...] = jnp.broadcast_to(total, cnt_ref.shape)

    info = jnp.zeros((tm, LANES), F32)
    for i, v in enumerate((e0.astype(F32), e1.astype(F32), g0, g1, rank0, rank1)):
        info = jnp.where(lane == i, v, info)
    info_ref[...] = info


def _router(x, sc, sh, w_router, *, tm=256):
    b, s, d = x.shape
    tm = min(tm, s)
    nt = s // tm
    mod = pl.BlockSpec((1, 1, d), lambda bi, i: (bi, 0, 0))
    return pl.pallas_call(
        _router_kernel,
        grid=(b, nt),
        in_specs=[pl.BlockSpec((1, tm, d), lambda bi, i: (bi, i, 0)), mod, mod,
                  pl.BlockSpec((d, LANES), lambda bi, i: (0, 0))],
        out_specs=[pl.BlockSpec((tm, d), lambda bi, i: (bi * nt + i, 0)),
                   pl.BlockSpec((tm, LANES), lambda bi, i: (bi * nt + i, 0)),
                   pl.BlockSpec((8, LANES), lambda bi, i: (0, 0))],
        out_shape=[jax.ShapeDtypeStruct((b * s, d), F32),
                   jax.ShapeDtypeStruct((b * s, LANES), F32),
                   jax.ShapeDtypeStruct((8, LANES), F32)],
        scratch_shapes=[pltpu.VMEM((8, LANES), F32)],
        compiler_params=_cparams(("arbitrary", "arbitrary")),
        name="moe_router",
    )(x, sc, sh, jnp.pad(w_router, ((0, 0), (0, LANES - N_EXPERTS))))


def _row_copy(src_ref, src_row, dst_ref, dst_row, sem):
    return pltpu.make_async_copy(src_ref.at[pl.ds(src_row, 1)], dst_ref.at[pl.ds(dst_row, 1)], sem)


def _dispatch_kernel(dest_ref, h_ref, init_ref, xs_ref, sem):
    del init_ref
    tm = h_ref.shape[0]

    def issue(i, carry):
        for k in range(TOP_K):
            _row_copy(h_ref, i, xs_ref, dest_ref[TOP_K * i + k], sem).start(priority=k)
        return carry

    def drain(i, carry):
        for k in range(TOP_K):
            _row_copy(h_ref, 0, xs_ref, 0, sem).wait()
        return carry

    lax.fori_loop(0, tm, issue, 0, unroll=8)
    lax.fori_loop(0, tm, drain, 0, unroll=8)


def _dispatch(h, dest, n_rows, *, tm=256):
    t, d = h.shape
    tm = min(tm, t)
    return pl.pallas_call(
        _dispatch_kernel,
        grid=(t // tm,),
        in_specs=[pl.BlockSpec((TOP_K * tm,), lambda i: (i,), memory_space=pltpu.SMEM),
                  pl.BlockSpec((tm, d), lambda i: (i, 0)),
                  pl.BlockSpec(memory_space=pl.ANY)],
        out_specs=pl.BlockSpec(memory_space=pl.ANY),
        out_shape=jax.ShapeDtypeStruct((n_rows, d), F32),
        scratch_shapes=[pltpu.SemaphoreType.DMA],
        input_output_aliases={2: 0},
        compiler_params=_cparams(("arbitrary",)),
        name="moe_dispatch",
    )(dest, h, jnp.zeros((n_rows, d), F32))


def _expert_kernel(be_ref, bv_ref, xs_ref, wg_ref, wu_ref, wd_ref, y_ref, *, f_chunk):
    del be_ref
    i = pl.program_id(0)

    @pl.when(bv_ref[i] != 0)
    def _():
        x = xs_ref[...].astype(BF16)
        f = wg_ref.shape[2]
        acc = jnp.zeros(y_ref.shape, F32)
        for c0 in range(0, f, f_chunk):
            gate = jnp.dot(x, wg_ref[0, :, c0:c0 + f_chunk], preferred_element_type=F32)
            up = jnp.dot(x, wu_ref[0, :, c0:c0 + f_chunk], preferred_element_type=F32)
            acc = acc + jnp.dot((_silu(gate) * up).astype(BF16), wd_ref[0, c0:c0 + f_chunk, :],
                                preferred_element_type=F32)
        y_ref[...] = acc

    @pl.when(bv_ref[i] == 0)
    def _():
        y_ref[...] = jnp.zeros_like(y_ref)


def _experts(xs, block_exp, block_valid, wg, wu, wd, *, f_chunk=512):
    n_rows, d = xs.shape
    f = wg.shape[2]
    n_blocks = n_rows // MOE_ROWS
    wspec = lambda shape: pl.BlockSpec(shape, lambda i, be, bv: (be[i], 0, 0),
                                       pipeline_mode=pl.Buffered(1))
    return pl.pallas_call(
        functools.partial(_expert_kernel, f_chunk=f_chunk),
        grid_spec=pltpu.PrefetchScalarGridSpec(
            num_scalar_prefetch=2,
            grid=(n_blocks,),
            in_specs=[pl.BlockSpec((MOE_ROWS, d), lambda i, be, bv: (i, 0)),
                      wspec((1, d, f)), wspec((1, d, f)), wspec((1, f, d))],
            out_specs=pl.BlockSpec((MOE_ROWS, d), lambda i, be, bv: (i, 0))),
        out_shape=jax.ShapeDtypeStruct((n_rows, d), F32),
        compiler_params=_cparams(("arbitrary",)),
        name="moe_experts",
    )(block_exp, block_valid, xs, wg, wu, wd)


def _combine_kernel(dest_ref, next_dest_ref, info_ref, x_ref, g_ref, fg_ref, y_ref, o_ref, buf_ref,
                    sem):
    tm = x_ref.shape[0]
    step = pl.program_id(0)
    slot = step % 2

    def gather(idx_ref, to_slot):
        def issue(i, carry):
            for k in range(TOP_K):
                _row_copy(y_ref, idx_ref[TOP_K * i + k], buf_ref.at[to_slot, k], i,
                          sem.at[to_slot]).start(priority=k)
            return carry
        lax.fori_loop(0, tm, issue, 0, unroll=8)

    @pl.when(step == 0)
    def _():
        gather(dest_ref, 0)

    @pl.when(step + 1 < pl.num_programs(0))
    def _():
        gather(next_dest_ref, 1 - slot)

    def drain(i, carry):
        for k in range(TOP_K):
            _row_copy(y_ref, 0, buf_ref.at[slot, k], 0, sem.at[slot]).wait()
        return carry

    lax.fori_loop(0, tm, drain, 0, unroll=8)
    info = info_ref[...]
    ffn = info[:, 2:3] * buf_ref[slot, 0] + info[:, 3:4] * buf_ref[slot, 1]
    x = x_ref[...] + g_ref[0] * ffn
    o_ref[...] = x * lax.rsqrt(jnp.mean(x * x, axis=-1, keepdims=True) + EPS) * fg_ref[...]


def _combine(dest, info, x2d, g, final_g, y, seq, *, tm=256):
    t, d = x2d.shape
    tm = min(tm, seq)
    per_seq = seq // tm
    n_tiles = t // tm
    return pl.pallas_call(
        _combine_kernel,
        grid=(n_tiles,),
        in_specs=[pl.BlockSpec((TOP_K * tm,), lambda i: (i,), memory_space=pltpu.SMEM),
                  pl.BlockSpec((TOP_K * tm,), lambda i: (jnp.minimum(i + 1, n_tiles - 1),),
                               memory_space=pltpu.SMEM),
                  pl.BlockSpec((tm, LANES), lambda i: (i, 0)),
                  pl.BlockSpec((tm, d), lambda i: (i, 0)),
                  pl.BlockSpec((1, 1, d), lambda i: (i // per_seq, 0, 0)),
                  pl.BlockSpec((1, d), lambda i: (0, 0)),
                  pl.BlockSpec(memory_space=pl.ANY)],
        out_specs=pl.BlockSpec((tm, d), lambda i: (i, 0)),
        out_shape=jax.ShapeDtypeStruct((t, d), F32),
        scratch_shapes=[pltpu.VMEM((2, TOP_K, tm, d), F32), pltpu.SemaphoreType.DMA((2,))],
        compiler_params=_cparams(("arbitrary",)),
        name="moe_combine",
    )(dest, dest, info, x2d, g, final_g[None, :], y)


def _moe_plan(info, counts):
    e = info[:, 0:TOP_K].astype(jnp.int32)
    rank = info[:, 4:4 + TOP_K].astype(jnp.int32)
    cnt = counts[0, :N_EXPERTS].astype(jnp.int32)
    padded = (cnt + MOE_ROWS - 1) // MOE_ROWS * MOE_ROWS
    pends = jnp.cumsum(padded)
    pstarts = pends - padded
    start_of = jnp.zeros_like(e)
    for j in range(N_EXPERTS):
        start_of = jnp.where(e == j, pstarts[j], start_of)
    dest = (start_of + rank).reshape(-1)
    n_blocks = (TOP_K * info.shape[0]) // MOE_ROWS + N_EXPERTS
    block_start = jnp.arange(n_blocks, dtype=jnp.int32) * MOE_ROWS
    valid = block_start < pends[-1]
    last_start = jnp.maximum(pends[-1] - MOE_ROWS, 0)
    block_exp = jnp.searchsorted(pends, jnp.where(valid, block_start, last_start), side='right')
    block_exp = jnp.minimum(block_exp, N_EXPERTS - 1).astype(jnp.int32)
    return dest, block_exp, valid.astype(jnp.int32), n_blocks * MOE_ROWS


def kernel(x, c, ada_w0, ada_b0, m_w_in, m_conv_w, m_conv_b, m_dt_bias, m_a_log, m_d_skip, m_norm_g, m_w_out, ffn_w_gate, ffn_w_up, ffn_w_down, ada_w1, ada_b1, a_w_qkv, a_lam_q1, a_lam_k1, a_lam_q2, a_lam_k2, a_subln_g, a_w_o, moe_w_router, moe_w_gate, moe_w_up, moe_w_down, final_g):
    b, s, d = x.shape
    bf = lambda w: w.astype(BF16)
    c8 = jnp.pad(c, ((0, 8 - b), (0, 0)))

    def mods(w, bias):
        mod = _adaln(c8, w, bias[None, :])[:b]
        return [m[:, None, :] for m in jnp.split(mod, 6, axis=-1)]

    sh1, sc1, g1, sh2, sc2, g2 = mods(ada_w0, ada_b0)
    w_dt = jnp.pad(m_w_in[:, M_MAIN_W:], ((0, 0), (0, LANES - M_HEADS)))
    main, dt_raw = _norm_proj(x, sc1, sh1, bf(m_w_in[:, :M_MAIN_W]), bf(w_dt))
    y = _ssd(main, dt_raw, m_conv_w, m_conv_b, m_dt_bias, m_a_log, m_d_skip, m_norm_g)
    x = _proj_res(y, bf(m_w_out), x, g1)
    x = _swiglu(x, sc2, sh2, g2, bf(ffn_w_gate), bf(ffn_w_up), bf(ffn_w_down))

    sh1, sc1, g1, sh2, sc2, g2 = mods(ada_w1, ada_b1)
    lambda_init = 0.8 - 0.6 * math.exp(-0.3 * 1)
    qkv = _norm_proj(x, sc1, sh1, bf(a_w_qkv), head_major=True)
    lam_pack = jnp.pad(jnp.stack([a_lam_q1, a_lam_k1, a_lam_q2, a_lam_k2]),
                       ((0, 4), (0, LANES - A_HEAD_DIM)))
    att = _attention(qkv, lam_pack, a_subln_g, lambda_init)
    x = _proj_res(att, bf(a_w_o), x, g1)

    h, info, counts = _router(x, sc2, sh2, moe_w_router)
    dest, block_exp, block_valid, n_rows = _moe_plan(info, counts)
    xs = _dispatch(h, dest, n_rows)
    ye = _experts(xs, block_exp, block_valid, bf(moe_w_gate), bf(moe_w_up), bf(moe_w_down))
    out = _combine(dest, info, x.reshape(b * s, d), g2, final_g, ye, s)
    return out.reshape(b, s, d)
```

```python
import functools
import math

import jax
import jax.numpy as jnp
from jax import lax
from jax.experimental import pallas as pl
from jax.experimental.pallas import tpu as pltpu

F32 = jnp.float32
BF16 = jnp.bfloat16
HIGHEST = lax.Precision.HIGHEST

D_MODEL = 1024
EPS = 1e-6
M_D_INNER = 2048
M_HEAD_DIM = 64
M_HEADS = 32
M_GROUPS = 8
M_HPG = 4
M_STATE = 128
M_CONV = 4
M_CHUNK = 128
M_CONV_COLS = 512
M_WIN_PAD = 16
M_GROUP_W = M_HPG * M_HEAD_DIM
M_BC_W = 2 * M_GROUPS * M_STATE
M_CONV_DIM = M_D_INNER + M_BC_W
M_MAIN_W = M_D_INNER + M_CONV_DIM
A_HEADS = 8
A_HEAD_DIM = 64
A_V_DIM = 128
A_TQ = 512
A_TK = 512
FFN_DIM = 2816
N_EXPERTS = 8
TOP_K = 2
EXPERT_DIM = 3584
MOE_ROWS = 512
LANES = 128
VMEM_LIMIT = 56 * 1024 * 1024


def _silu(v):
    return v * jax.nn.sigmoid(v)


def _norm_mod(x, sc, sh):
    ms = jnp.mean(x * x, axis=-1, keepdims=True)
    return x * lax.rsqrt(ms + EPS) * (1.0 + sc) + sh


def _cparams(sem):
    return pltpu.CompilerParams(dimension_semantics=sem, vmem_limit_bytes=VMEM_LIMIT)


def _resident(shape):
    nd = len(shape)
    return pl.BlockSpec(shape, lambda *_: (0,) * nd, pipeline_mode=pl.Buffered(1))


def _adaln_kernel(c_ref, w_ref, b_ref, o_ref):
    o_ref[...] = jnp.dot(_silu(c_ref[...]), w_ref[...], preferred_element_type=F32,
                         precision=HIGHEST) + b_ref[...]


def _adaln(c8, w, b):
    rows, d = c8.shape
    n = w.shape[1]
    tn = 1024
    return pl.pallas_call(
        _adaln_kernel,
        grid=(n // tn,),
        in_specs=[pl.BlockSpec((rows, d), lambda j: (0, 0)),
                  pl.BlockSpec((d, tn), lambda j: (0, j)),
                  pl.BlockSpec((1, tn), lambda j: (0, j))],
        out_specs=pl.BlockSpec((rows, tn), lambda j: (0, j)),
        out_shape=jax.ShapeDtypeStruct((rows, n), F32),
        compiler_params=_cparams(("arbitrary",)),
        name="adaln",
    )(c8, w, b)


def _norm_proj_kernel(x_ref, sc_ref, sh_ref, w_ref, *rest, n_chunk, has_aux):
    if has_aux:
        waux_ref, o_ref, aux_ref = rest
    else:
        (o_ref,) = rest
    h = _norm_mod(x_ref[0], sc_ref[0], sh_ref[0]).astype(BF16)
    n = w_ref.shape[1]
    for c0 in range(0, n, n_chunk):
        res = jnp.dot(h, w_ref[:, c0:c0 + n_chunk],
                      preferred_element_type=F32).astype(o_ref.dtype)
        if len(o_ref.shape) == 3:
            o_ref[0, :, c0:c0 + n_chunk] = res
        else:
            for l0 in range(0, n_chunk, LANES):
                o_ref[0, (c0 + l0) // LANES] = res[:, l0:l0 + LANES]
    if has_aux:
        aux_ref[0] = jnp.dot(h, waux_ref[...], preferred_element_type=F32)


def _norm_proj(x, sc, sh, w, w_aux=None, *, head_major=False, tm=512, n_chunk=1024):
    b, s, d = x.shape
    n = w.shape[1]
    tm = min(tm, s)
    in_specs = [pl.BlockSpec((1, tm, d), lambda bi, i: (bi, i, 0)),
                pl.BlockSpec((1, 1, d), lambda bi, i: (bi, 0, 0)),
                pl.BlockSpec((1, 1, d), lambda bi, i: (bi, 0, 0)),
                _resident((d, n))]
    if head_major:
        out_specs = [pl.BlockSpec((1, n // LANES, tm, LANES), lambda bi, i: (bi, 0, i, 0))]
        out_shape = [jax.ShapeDtypeStruct((b, n // LANES, s, LANES), BF16)]
    else:
        out_specs = [pl.BlockSpec((1, tm, n), lambda bi, i: (bi, i, 0))]
        out_shape = [jax.ShapeDtypeStruct((b, s, n), BF16)]
    args = [x, sc, sh, w]
    if w_aux is not None:
        na = w_aux.shape[1]
        in_specs.append(_resident((d, na)))
        out_specs.append(pl.BlockSpec((1, tm, na), lambda bi, i: (bi, i, 0)))
        out_shape.append(jax.ShapeDtypeStruct((b, s, na), F32))
        args.append(w_aux)
    out = pl.pallas_call(
        functools.partial(_norm_proj_kernel, n_chunk=n_chunk, has_aux=w_aux is not None),
        grid=(b, s // tm),
        in_specs=in_specs, out_specs=out_specs, out_shape=out_shape,
        compiler_params=_cparams(("parallel", "parallel")),
        name="norm_proj",
    )(*args)
    return out if w_aux is not None else out[0]


def _proj_res_kernel(a_ref, w_ref, x_ref, g_ref, o_ref):
    if len(a_ref.shape) == 3:
        a = a_ref[0]
    else:
        a = jnp.concatenate([a_ref[0, h] for h in range(a_ref.shape[1])], axis=1)
    o_ref[0] = x_ref[0] + g_ref[0] * jnp.dot(a, w_ref[...], preferred_element_type=F32)


def _proj_res(a, w, x, g, *, tm=512):
    b, s, _ = x.shape
    k, d = w.shape
    tm = min(tm, s)
    if len(a.shape) == 3:
        a_spec = pl.BlockSpec((1, tm, k), lambda bi, i: (bi, i, 0))
    else:
        a_spec = pl.BlockSpec((1, k // LANES, tm, LANES), lambda bi, i: (bi, 0, i, 0))
    return pl.pallas_call(
        _proj_res_kernel,
        grid=(b, s // tm),
        in_specs=[a_spec,
                  _resident((k, d)),
                  pl.BlockSpec((1, tm, d), lambda bi, i: (bi, i, 0)),
                  pl.BlockSpec((1, 1, d), lambda bi, i: (bi, 0, 0))],
        out_specs=pl.BlockSpec((1, tm, d), lambda bi, i: (bi, i, 0)),
        out_shape=jax.ShapeDtypeStruct((b, s, d), F32),
        compiler_params=_cparams(("parallel", "parallel")),
        name="proj_res",
    )(a, w, x, g)


def _softplus(v):
    return jnp.maximum(v, 0.0) + jnp.log1p(jnp.exp(-jnp.abs(v)))


def _ssd_kernel(z_ref, x_ref, bc_ref, dt_ref, convw_ref, convb_ref, dtb_ref, alog_ref, dskip_ref,
                ng_ref, expand_ref, shift_ref, y_ref, win_ref, xbc_ref, state_ref):
    q = M_CHUNK
    gw = M_GROUP_W

    @pl.when(pl.program_id(1) == 0)
    def _():
        win_ref[0:M_WIN_PAD, :] = jnp.zeros((M_WIN_PAD, M_CONV_DIM), BF16)
        state_ref[...] = jnp.zeros_like(state_ref)

    win_ref[M_WIN_PAD:M_WIN_PAD + q, 0:M_D_INNER] = x_ref[0]
    win_ref[M_WIN_PAD:M_WIN_PAD + q, M_D_INNER:M_CONV_DIM] = bc_ref[0]
    shift = shift_ref[...]

    def conv_block(j):
        cols = slice(j * M_CONV_COLS, (j + 1) * M_CONV_COLS)
        shifted = jnp.dot(shift, win_ref[:, cols], preferred_element_type=F32)
        acc = (convb_ref[:, cols] + convw_ref[M_CONV - 1:M_CONV, cols]
               * win_ref[M_WIN_PAD:M_WIN_PAD + q, cols].astype(F32))
        for k in range(M_CONV - 1):
            acc = acc + convw_ref[k:k + 1, cols] * shifted[k * q:(k + 1) * q, :]
        xbc_ref[:, cols] = _silu(acc)

    for j in (4, 6, 0):
        conv_block(j)
    conv_after_group = {0: 1, 1: 5, 2: 7, 3: 2, 4: 3}

    dt = _softplus(dt_ref[0] + dtb_ref[...])
    da = dt * (-jnp.exp(alog_ref[...]))
    ri = lax.broadcasted_iota(jnp.int32, (q, q), 0)
    ci = lax.broadcasted_iota(jnp.int32, (q, q), 1)
    tril = ci <= ri
    acs = jnp.dot(tril.astype(F32), da, preferred_element_type=F32, precision=HIGHEST)
    acs_t = acs.T
    dt_t = dt.T
    a_last = acs[q - 1:q, :]

    expand = expand_ref[...]

    def per_channel(v):
        hi = v.astype(BF16)
        lo = (v - hi.astype(F32)).astype(BF16)
        return (jnp.dot(hi, expand, preferred_element_type=F32)
                + jnp.dot(lo, expand, preferred_element_type=F32))

    decay_in = per_channel(jnp.exp(acs))
    decay_out_dt = per_channel(jnp.exp(a_last - acs) * dt)
    decay_chunk = decay_in[q - 1:q, :]

    lane = lax.broadcasted_iota(jnp.int32, (q, LANES), 1)
    for g in range(M_GROUPS):
        b16 = xbc_ref[:, M_D_INNER + g * M_STATE:M_D_INNER + (g + 1) * M_STATE].astype(BF16)
        c16 = xbc_ref[:, M_D_INNER + (M_GROUPS + g) * M_STATE:
                  M_D_INNER + (M_GROUPS + g + 1) * M_STATE].astype(BF16)
        xs = xbc_ref[:, g * gw:(g + 1) * gw]
        xs16 = xs.astype(BF16)
        cb = lax.dot_general(c16, b16, (((1,), (1,)), ((), ())), preferred_element_type=F32)
        state = state_ref[g]
        y_off = (jnp.dot(c16, state.astype(BF16), preferred_element_type=F32)
                 * decay_in[:, g * gw:(g + 1) * gw])
        pairs = []
        for p in range(M_HPG // 2):
            ms = []
            for r in range(2):
                h = g * M_HPG + 2 * p + r
                seg = acs[:, h:h + 1] - acs_t[h:h + 1, :]
                decay = jnp.exp(jnp.where(tril, seg, -jnp.inf))
                ms.append((cb * decay * dt_t[h:h + 1, :]).astype(BF16))
            x2 = xs16[:, p * LANES:(p + 1) * LANES]
            zero = jnp.zeros_like(x2)
            rhs = jnp.concatenate([jnp.where(lane < M_HEAD_DIM, x2, zero),
                                   jnp.where(lane >= M_HEAD_DIM, x2, zero)], axis=0)
            pairs.append(jnp.dot(jnp.concatenate(ms, axis=1), rhs, preferred_element_type=F32))
        y = jnp.concatenate(pairs, axis=1) + y_off + xs * dskip_ref[:, g * gw:(g + 1) * gw]

        xw = (xs * decay_out_dt[:, g * gw:(g + 1) * gw]).astype(BF16)
        state_ref[g] = (state * decay_chunk[:, g * gw:(g + 1) * gw]
                        + lax.dot_general(b16, xw, (((0,), (0,)), ((), ())),
                                          preferred_element_type=F32))

        y = y * _silu(z_ref[0, :, g * gw:(g + 1) * gw].astype(F32))
        ms2 = jnp.mean(y * y, axis=-1, keepdims=True)
        y_ref[0, :, g * gw:(g + 1) * gw] = (
            y * lax.rsqrt(ms2 + EPS) * ng_ref[:, g * gw:(g + 1) * gw]).astype(BF16)
        if g in conv_after_group:
            conv_block(conv_after_group[g])

    win_ref[0:M_WIN_PAD, :] = win_ref[q:q + M_WIN_PAD, :]


def _ssd(main, dt_raw, conv_w, conv_b, dt_bias, a_log, d_skip, norm_g):
    b, s, _ = main.shape
    nc = s // M_CHUNK
    pad = LANES - M_HEADS
    head = lax.broadcasted_iota(jnp.int32, (LANES, M_D_INNER), 0)
    chan = lax.broadcasted_iota(jnp.int32, (LANES, M_D_INNER), 1)
    expand = (chan // M_HEAD_DIM == head).astype(BF16)
    win_rows = M_WIN_PAD + M_CHUNK
    out_row = lax.broadcasted_iota(jnp.int32, ((M_CONV - 1) * M_CHUNK, win_rows), 0)
    win_row = lax.broadcasted_iota(jnp.int32, ((M_CONV - 1) * M_CHUNK, win_rows), 1)
    shift = (win_row == out_row % M_CHUNK + out_row // M_CHUNK + M_WIN_PAD - (M_CONV - 1)).astype(BF16)
    vec = lambda n: pl.BlockSpec((1, n), lambda bi, c: (0, 0))
    blk = lambda j: pl.BlockSpec((1, M_CHUNK, M_D_INNER), lambda bi, c: (bi, c, j))
    return pl.pallas_call(
        _ssd_kernel,
        grid=(b, nc),
        in_specs=[blk(0), blk(1), blk(2),
                  pl.BlockSpec((1, M_CHUNK, LANES), lambda bi, c: (bi, c, 0)),
                  pl.BlockSpec((M_CONV, M_CONV_DIM), lambda bi, c: (0, 0)),
                  vec(M_CONV_DIM), vec(LANES), vec(LANES), vec(M_D_INNER), vec(M_D_INNER),
                  pl.BlockSpec((LANES, M_D_INNER), lambda bi, c: (0, 0)),
                  pl.BlockSpec(shift.shape, lambda bi, c: (0, 0))],
        out_specs=pl.BlockSpec((1, M_CHUNK, M_D_INNER), lambda bi, c: (bi, c, 0)),
        out_shape=jax.ShapeDtypeStruct((b, s, M_D_INNER), BF16),
        scratch_shapes=[pltpu.VMEM((win_rows, M_CONV_DIM), BF16),
                        pltpu.VMEM((M_CHUNK, M_CONV_DIM), F32),
                        pltpu.VMEM((M_GROUPS, M_STATE, M_GROUP_W), F32)],
        compiler_params=_cparams(("parallel", "arbitrary")),
        name="ssd",
    )(main, main, main, dt_raw, conv_w, conv_b[None, :],
      jnp.pad(dt_bias, (0, pad))[None, :], jnp.pad(a_log, (0, pad))[None, :],
      jnp.repeat(d_skip, M_HEAD_DIM)[None, :], norm_g[None, :], expand, shift)


def _swiglu_kernel(x_ref, sc_ref, sh_ref, g_ref, wg_ref, wu_ref, wd_ref, o_ref, *, f_chunk):
    x = x_ref[0]
    h = _norm_mod(x, sc_ref[0], sh_ref[0]).astype(BF16)
    f = wg_ref.shape[1]
    acc = jnp.zeros(x.shape, F32)
    for c0 in range(0, f, f_chunk):
        c1 = min(c0 + f_chunk, f)
        gate = jnp.dot(h, wg_ref[:, c0:c1], preferred_element_type=F32)
        up = jnp.dot(h, wu_ref[:, c0:c1], preferred_element_type=F32)
        acc = acc + jnp.dot((_silu(gate) * up).astype(BF16), wd_ref[c0:c1, :],
                            preferred_element_type=F32)
    o_ref[0] = x + g_ref[0] * acc


def _swiglu(x, sc, sh, g, wg, wu, wd, *, tm=512, f_chunk=512):
    b, s, d = x.shape
    f = wg.shape[1]
    tm = min(tm, s)
    mod = pl.BlockSpec((1, 1, d), lambda bi, i: (bi, 0, 0))
    return pl.pallas_call(
        functools.partial(_swiglu_kernel, f_chunk=f_chunk),
        grid=(b, s // tm),
        in_specs=[pl.BlockSpec((1, tm, d), lambda bi, i: (bi, i, 0)), mod, mod, mod,
                  _resident((d, f)), _resident((d, f)), _resident((f, d))],
        out_specs=pl.BlockSpec((1, tm, d), lambda bi, i: (bi, i, 0)),
        out_shape=jax.ShapeDtypeStruct((b, s, d), F32),
        compiler_params=_cparams(("parallel", "parallel")),
        name="swiglu",
    )(x, sc, sh, g, wg, wu, wd)


A_V_ROWS = A_V_DIM + 16


def _attn_kernel(q_ref, k_ref, v_ref, lam_ref, g_ref, o_ref, vt_ref, qm_ref, st_ref, p_ref, m_ref,
                 al_ref, acc_ref, *, tq, tk, lambda_init):
    qi = pl.program_id(2)

    @pl.when(qi == 0)
    def _():
        for j in range(vt_ref.shape[0]):
            vt_ref[j, 0:A_V_DIM, :] = v_ref[0, 0, j * tk:(j + 1) * tk, :].astype(F32).T.astype(BF16)
            vt_ref[j, A_V_DIM:A_V_ROWS, :] = jnp.ones((A_V_ROWS - A_V_DIM, tk), BF16)

    lane = lax.broadcasted_iota(jnp.int32, (tq, LANES), 1)
    q = q_ref[0, 0]
    q = (q.astype(F32) * (A_HEAD_DIM ** -0.5)).astype(BF16)
    zero = jnp.zeros_like(q)
    qm_ref[0] = jnp.where(lane < A_HEAD_DIM, q, zero)
    qm_ref[1] = jnp.where(lane >= A_HEAD_DIM, q, zero)
    m_ref[...] = jnp.full(m_ref.shape, -jnp.inf, F32)
    acc_ref[...] = jnp.zeros(acc_ref.shape, F32)

    last = (qi * tq) // tk

    def scores(blk, j):
        start = pl.multiple_of(blk * tk, tk)
        st_ref[j] = lax.dot_general(k_ref[0, 0, pl.ds(start, tk), :], qm_ref[j],
                                    (((1,), (1,)), ((), ())), preferred_element_type=F32)

    def softmax(blk, j, masked):
        st = st_ref[j]
        if masked:
            key = blk * tk + lax.broadcasted_iota(jnp.int32, (tk, tq), 0)
            qry = qi * tq + lax.broadcasted_iota(jnp.int32, (tk, tq), 1)
            st = jnp.where(key <= qry, st, -jnp.inf)
        m_old = m_ref[j]
        top = st
        while top.shape[0] > 32:
            half = top.shape[0] // 2
            top = jnp.maximum(top[:half], top[half:])
        m_new = jnp.maximum(m_old, jnp.max(top, axis=0, keepdims=True))
        al_ref[j] = jnp.exp(m_old - m_new)
        p_ref[j] = jnp.exp(st - m_new).astype(BF16)
        m_ref[j] = m_new

    def weighted_values(blk, j):
        acc_ref[j] = al_ref[j] * acc_ref[j] + jnp.dot(vt_ref[blk], p_ref[j],
                                                      preferred_element_type=F32)

    for j in range(2):
        scores(0, j)

    def visit(blk, diagonal, first=False):
        for j in range(2):
            if not first:
                weighted_values(blk - 1, j)
            softmax(blk, j, diagonal)
            if not diagonal:
                scores(blk + 1, j)

    @pl.when(last == 0)
    def _():
        visit(0, True, first=True)

    @pl.when(last > 0)
    def _():
        visit(0, False, first=True)

        def body(pair, carry):
            visit(2 * pair + 1, False)
            visit(2 * pair + 2, False)
            return carry

        lax.fori_loop(0, (last - 1) // 2, body, 0)

        @pl.when(last % 2 == 0)
        def _():
            visit(last - 1, False)

        visit(last, True)

    for j in range(2):
        weighted_values(last, j)

    lam_v = lam_ref[...]
    lam = (jnp.exp(jnp.sum(lam_v[0:1] * lam_v[1:2], axis=-1, keepdims=True))
           - jnp.exp(jnp.sum(lam_v[2:3] * lam_v[3:4], axis=-1, keepdims=True)) + lambda_init)
    acc0 = acc_ref[0]
    acc1 = acc_ref[1]
    o = (acc0[0:A_V_DIM] / acc0[A_V_DIM:A_V_DIM + 1]
         - lam * (acc1[0:A_V_DIM] / acc1[A_V_DIM:A_V_DIM + 1])).T
    o = o * lax.rsqrt(jnp.mean(o * o, axis=-1, keepdims=True) + EPS)
    o_ref[0, 0] = (o * g_ref[...] * (1.0 - lambda_init)).astype(BF16)


def _attention(qkv, lam_pack, subln_g, lambda_init):
    b, _, s, _ = qkv.shape
    tq = min(A_TQ, s)
    tk = min(A_TK, s)
    return pl.pallas_call(
        functools.partial(_attn_kernel, tq=tq, tk=tk, lambda_init=lambda_init),
        grid=(b, A_HEADS, s // tq),
        in_specs=[pl.BlockSpec((1, 1, tq, LANES), lambda bi, h, i: (bi, h, i, 0)),
                  pl.BlockSpec((1, 1, s, LANES), lambda bi, h, i: (bi, A_HEADS + h, 0, 0)),
                  pl.BlockSpec((1, 1, s, LANES), lambda bi, h, i: (bi, 2 * A_HEADS + h, 0, 0)),
                  pl.BlockSpec((8, LANES), lambda bi, h, i: (0, 0)),
                  pl.BlockSpec((1, LANES), lambda bi, h, i: (0, 0))],
        out_specs=pl.BlockSpec((1, 1, tq, LANES), lambda bi, h, i: (bi, h, i, 0)),
        out_shape=jax.ShapeDtypeStruct((b, A_HEADS, s, A_V_DIM), BF16),
        scratch_shapes=[pltpu.VMEM((s // tk, A_V_ROWS, tk), BF16),
                        pltpu.VMEM((2, tq, LANES), BF16),
                        pltpu.VMEM((2, tk, tq), F32),
                        pltpu.VMEM((2, tk, tq), BF16),
                        pltpu.VMEM((2, 1, tq), F32),
                        pltpu.VMEM((2, 1, tq), F32),
                        pltpu.VMEM((2, A_V_ROWS, tq), F32)],
        compiler_params=_cparams(("parallel", "parallel", "arbitrary")),
        name="diff_attn",
    )(qkv, qkv, qkv, lam_pack, subln_g[None, :])


def _router_kernel(x_ref, sc_ref, sh_ref, wr_ref, h_ref, info_ref, cnt_ref, run_ref):
    tm = x_ref.shape[1]

    @pl.when((pl.program_id(0) == 0) & (pl.program_id(1) == 0))
    def _():
        run_ref[...] = jnp.zeros_like(run_ref)

    h = _norm_mod(x_ref[0], sc_ref[0], sh_ref[0])
    h_ref[...] = h
    h_hi = h.astype(BF16)
    h_lo = (h - h_hi.astype(F32)).astype(BF16)
    logits = (jnp.dot(h_hi, wr_ref[0], preferred_element_type=F32)
              + jnp.dot(h_lo, wr_ref[0], preferred_element_type=F32)
              + jnp.dot(h_hi, wr_ref[1], preferred_element_type=F32))
    lane = lax.broadcasted_iota(jnp.int32, (tm, LANES), 1)
    lg = jnp.where(lane < N_EXPERTS, logits, -jnp.inf)
    m0 = jnp.max(lg, axis=-1, keepdims=True)
    e0 = jnp.min(jnp.where(lg == m0, lane, LANES), axis=-1, keepdims=True)
    lg1 = jnp.where(lane == e0, -jnp.inf, lg)
    m1 = jnp.max(lg1, axis=-1, keepdims=True)
    e1 = jnp.min(jnp.where(lg1 == m1, lane, LANES), axis=-1, keepdims=True)
    ex = jnp.exp(m1 - m0)
    g0 = 1.0 / (1.0 + ex)
    g1 = ex / (1.0 + ex)

    pick0 = lane == e0
    pick1 = lane == e1
    onehot = (pick0 | pick1).astype(BF16)
    ri = lax.broadcasted_iota(jnp.int32, (tm, tm), 0)
    ci = lax.broadcasted_iota(jnp.int32, (tm, tm), 1)
    before = jnp.dot((ci < ri).astype(BF16), onehot, preferred_element_type=F32) + run_ref[0:1, :]
    rank0 = jnp.sum(jnp.where(pick0, before, 0.0), axis=-1, keepdims=True)
    rank1 = jnp.sum(jnp.where(pick1, before, 0.0), axis=-1, keepdims=True)
    total = run_ref[0:1, :] + jnp.sum(onehot.astype(F32), axis=0, keepdims=True)
    run_ref[...] = jnp.broadcast_to(total, run_ref.shape)
    cnt_ref[...] = jnp.broadcast_to(total, cnt_ref.shape)

    info = jnp.zeros((tm, LANES), F32)
    for i, v in enumerate((e0.astype(F32), e1.astype(F32), g0, g1, rank0, rank1)):
        info = jnp.where(lane == i, v, info)
    info_ref[...] = info


def _router(x, sc, sh, w_router, *, tm=256):
    b, s, d = x.shape
    tm = min(tm, s)
    nt = s // tm
    mod = pl.BlockSpec((1, 1, d), lambda bi, i: (bi, 0, 0))
    w = jnp.pad(w_router, ((0, 0), (0, LANES - N_EXPERTS)))
    w_hi = w.astype(BF16)
    w_split = jnp.stack([w_hi, (w - w_hi.astype(F32)).astype(BF16)])
    return pl.pallas_call(
        _router_kernel,
        grid=(b, nt),
        in_specs=[pl.BlockSpec((1, tm, d), lambda bi, i: (bi, i, 0)), mod, mod,
                  pl.BlockSpec((2, d, LANES), lambda bi, i: (0, 0, 0))],
        out_specs=[pl.BlockSpec((tm, d), lambda bi, i: (bi * nt + i, 0)),
                   pl.BlockSpec((tm, LANES), lambda bi, i: (bi * nt + i, 0)),
                   pl.BlockSpec((8, LANES), lambda bi, i: (0, 0))],
        out_shape=[jax.ShapeDtypeStruct((b * s, d), F32),
                   jax.ShapeDtypeStruct((b * s, LANES), F32),
                   jax.ShapeDtypeStruct((8, LANES), F32)],
        scratch_shapes=[pltpu.VMEM((8, LANES), F32)],
        compiler_params=_cparams(("arbitrary", "arbitrary")),
        name="moe_router",
    )(x, sc, sh, w_split)


def _row_copy(src_ref, src_row, dst_ref, dst_row, sem):
    return pltpu.make_async_copy(src_ref.at[pl.ds(src_row, 1)], dst_ref.at[pl.ds(dst_row, 1)], sem)


def _dispatch_kernel(dest_ref, pend_ref, h_ref, xs_ref, zero_ref, sem, zero_sem):
    tm = h_ref.shape[0]

    @pl.when(pl.program_id(0) == 0)
    def _():
        zero_ref[...] = jnp.zeros(zero_ref.shape, F32)

        def fill(e):
            end = pend_ref[e]
            nonempty = end > (pend_ref[e - 1] if e else 0)
            start = pl.multiple_of(jnp.maximum(end - MOE_ROWS, 0), MOE_ROWS)
            return nonempty, pltpu.make_async_copy(zero_ref, xs_ref.at[pl.ds(start, MOE_ROWS)],
                                                   zero_sem)

        for e in range(N_EXPERTS):
            nonempty, copy = fill(e)
            pl.when(nonempty)(copy.start)
        for e in range(N_EXPERTS):
            nonempty, copy = fill(e)
            pl.when(nonempty)(copy.wait)

    def issue(i, carry):
        for k in range(TOP_K):
            _row_copy(h_ref, i, xs_ref, dest_ref[TOP_K * i + k], sem).start(priority=k)
        return carry

    def drain(i, carry):
        for k in range(TOP_K):
            _row_copy(h_ref, 0, xs_ref, 0, sem).wait()
        return carry

    lax.fori_loop(0, tm, issue, 0, unroll=8)
    lax.fori_loop(0, tm, drain, 0, unroll=8)


def _dispatch(h, dest, pends, n_rows, *, tm=256):
    t, d = h.shape
    tm = min(tm, t)
    return pl.pallas_call(
        _dispatch_kernel,
        grid=(t // tm,),
        in_specs=[pl.BlockSpec((TOP_K * tm,), lambda i: (i,), memory_space=pltpu.SMEM),
                  pl.BlockSpec(memory_space=pltpu.SMEM),
                  pl.BlockSpec((tm, d), lambda i: (i, 0))],
        out_specs=pl.BlockSpec(memory_space=pl.ANY),
        out_shape=jax.ShapeDtypeStruct((n_rows, d), F32),
        scratch_shapes=[pltpu.VMEM((MOE_ROWS, d), F32), pltpu.SemaphoreType.DMA,
                        pltpu.SemaphoreType.DMA],
        compiler_params=_cparams(("arbitrary",)),
        name="moe_dispatch",
    )(dest, pends, h)


def _expert_kernel(be_ref, bv_ref, xs_ref, wg_ref, wu_ref, wd_ref, y_ref, *, f_chunk):
    del be_ref
    i = pl.program_id(0)

    @pl.when(bv_ref[i] != 0)
    def _():
        x = xs_ref[...].astype(BF16)
        f = wg_ref.shape[2]
        acc = jnp.zeros(y_ref.shape, F32)
        for c0 in range(0, f, f_chunk):
            gate = jnp.dot(x, wg_ref[0, :, c0:c0 + f_chunk], preferred_element_type=F32)
            up = jnp.dot(x, wu_ref[0, :, c0:c0 + f_chunk], preferred_element_type=F32)
            acc = acc + jnp.dot((_silu(gate) * up).astype(BF16), wd_ref[0, c0:c0 + f_chunk, :],
                                preferred_element_type=F32)
        y_ref[...] = acc

    @pl.when(bv_ref[i] == 0)
    def _():
        y_ref[...] = jnp.zeros_like(y_ref)


def _experts(xs, block_exp, block_valid, wg, wu, wd, *, f_chunk=512):
    n_rows, d = xs.shape
    f = wg.shape[2]
    n_blocks = n_rows // MOE_ROWS
    wspec = lambda shape: pl.BlockSpec(shape, lambda i, be, bv: (be[i], 0, 0),
                                       pipeline_mode=pl.Buffered(1))
    return pl.pallas_call(
        functools.partial(_expert_kernel, f_chunk=f_chunk),
        grid_spec=pltpu.PrefetchScalarGridSpec(
            num_scalar_prefetch=2,
            grid=(n_blocks,),
            in_specs=[pl.BlockSpec((MOE_ROWS, d), lambda i, be, bv: (i, 0)),
                      wspec((1, d, f)), wspec((1, d, f)), wspec((1, f, d))],
            out_specs=pl.BlockSpec((MOE_ROWS, d), lambda i, be, bv: (i, 0))),
        out_shape=jax.ShapeDtypeStruct((n_rows, d), F32),
        compiler_params=_cparams(("arbitrary",)),
        name="moe_experts",
    )(block_exp, block_valid, xs, wg, wu, wd)


def _combine_kernel(dest_ref, next_dest_ref, info_ref, x_ref, g_ref, fg_ref, y_ref, o_ref, buf_ref,
                    sem):
    tm = x_ref.shape[0]
    step = pl.program_id(0)
    slot = step % 2

    def gather(idx_ref, to_slot):
        def issue(i, carry):
            for k in range(TOP_K):
                _row_copy(y_ref, idx_ref[TOP_K * i + k], buf_ref.at[to_slot, k], i,
                          sem.at[to_slot]).start(priority=k)
            return carry
        lax.fori_loop(0, tm, issue, 0, unroll=8)

    @pl.when(step == 0)
    def _():
        gather(dest_ref, 0)

    @pl.when(step + 1 < pl.num_programs(0))
    def _():
        gather(next_dest_ref, 1 - slot)

    def drain(i, carry):
        for k in range(TOP_K):
            _row_copy(y_ref, 0, buf_ref.at[slot, k], 0, sem.at[slot]).wait()
        return carry

    lax.fori_loop(0, tm, drain, 0, unroll=8)
    info = info_ref[...]
    ffn = info[:, 2:3] * buf_ref[slot, 0] + info[:, 3:4] * buf_ref[slot, 1]
    x = x_ref[...] + g_ref[0] * ffn
    o_ref[...] = x * lax.rsqrt(jnp.mean(x * x, axis=-1, keepdims=True) + EPS) * fg_ref[...]


def _combine(dest, info, x2d, g, final_g, y, seq, *, tm=256):
    t, d = x2d.shape
    tm = min(tm, seq)
    per_seq = seq // tm
    n_tiles = t // tm
    return pl.pallas_call(
        _combine_kernel,
        grid=(n_tiles,),
        in_specs=[pl.BlockSpec((TOP_K * tm,), lambda i: (i,), memory_space=pltpu.SMEM),
                  pl.BlockSpec((TOP_K * tm,), lambda i: (jnp.minimum(i + 1, n_tiles - 1),),
                               memory_space=pltpu.SMEM),
                  pl.BlockSpec((tm, LANES), lambda i: (i, 0)),
                  pl.BlockSpec((tm, d), lambda i: (i, 0)),
                  pl.BlockSpec((1, 1, d), lambda i: (i // per_seq, 0, 0)),
                  pl.BlockSpec((1, d), lambda i: (0, 0)),
                  pl.BlockSpec(memory_space=pl.ANY)],
        out_specs=pl.BlockSpec((tm, d), lambda i: (i, 0)),
        out_shape=jax.ShapeDtypeStruct((t, d), F32),
        scratch_shapes=[pltpu.VMEM((2, TOP_K, tm, d), F32), pltpu.SemaphoreType.DMA((2,))],
        compiler_params=_cparams(("arbitrary",)),
        name="moe_combine",
    )(dest, dest, info, x2d, g, final_g[None, :], y)


def _moe_plan(info, counts):
    e = info[:, 0:TOP_K].astype(jnp.int32)
    rank = info[:, 4:4 + TOP_K].astype(jnp.int32)
    cnt = counts[0, :N_EXPERTS].astype(jnp.int32)
    padded = (cnt + MOE_ROWS - 1) // MOE_ROWS * MOE_ROWS
    pends = jnp.cumsum(padded)
    pstarts = pends - padded
    start_of = jnp.zeros_like(e)
    for j in range(N_EXPERTS):
        start_of = jnp.where(e == j, pstarts[j], start_of)
    dest = (start_of + rank).reshape(-1)
    n_blocks = (TOP_K * info.shape[0]) // MOE_ROWS + N_EXPERTS
    block_start = jnp.arange(n_blocks, dtype=jnp.int32) * MOE_ROWS
    valid = block_start < pends[-1]
    last_start = jnp.maximum(pends[-1] - MOE_ROWS, 0)
    start = jnp.where(valid, block_start, last_start)
    block_exp = jnp.sum((start[:, None] >= pends[None, :]).astype(jnp.int32), axis=1)
    block_exp = jnp.minimum(block_exp, N_EXPERTS - 1)
    return dest, pends, block_exp, valid.astype(jnp.int32), n_blocks * MOE_ROWS


def kernel(x, c, ada_w0, ada_b0, m_w_in, m_conv_w, m_conv_b, m_dt_bias, m_a_log, m_d_skip, m_norm_g, m_w_out, ffn_w_gate, ffn_w_up, ffn_w_down, ada_w1, ada_b1, a_w_qkv, a_lam_q1, a_lam_k1, a_lam_q2, a_lam_k2, a_subln_g, a_w_o, moe_w_router, moe_w_gate, moe_w_up, moe_w_down, final_g):
    b, s, d = x.shape
    bf = lambda w: w.astype(BF16)
    c8 = jnp.pad(c, ((0, 8 - b), (0, 0)))

    def mods(w, bias):
        mod = _adaln(c8, w, bias[None, :])[:b]
        return [m[:, None, :] for m in jnp.split(mod, 6, axis=-1)]

    sh1, sc1, g1, sh2, sc2, g2 = mods(ada_w0, ada_b0)
    w_dt = jnp.pad(m_w_in[:, M_MAIN_W:], ((0, 0), (0, LANES - M_HEADS)))
    main, dt_raw = _norm_proj(x, sc1, sh1, bf(m_w_in[:, :M_MAIN_W]), bf(w_dt))
    y = _ssd(main, dt_raw, m_conv_w, m_conv_b, m_dt_bias, m_a_log, m_d_skip, m_norm_g)
    x = _proj_res(y, bf(m_w_out), x, g1)
    x = _swiglu(x, sc2, sh2, g2, bf(ffn_w_gate), bf(ffn_w_up), bf(ffn_w_down))

    sh1, sc1, g1, sh2, sc2, g2 = mods(ada_w1, ada_b1)
    lambda_init = 0.8 - 0.6 * math.exp(-0.3 * 1)
    qkv = _norm_proj(x, sc1, sh1, bf(a_w_qkv), head_major=True)
    lam_pack = jnp.pad(jnp.stack([a_lam_q1, a_lam_k1, a_lam_q2, a_lam_k2]),
                       ((0, 4), (0, LANES - A_HEAD_DIM)))
    att = _attention(qkv, lam_pack, a_subln_g, lambda_init)
    x = _proj_res(att, bf(a_w_o), x, g1)

    h, info, counts = _router(x, sc2, sh2, moe_w_router)
    dest, pends, block_exp, block_valid, n_rows = _moe_plan(info, counts)
    xs = _dispatch(h, dest, pends, n_rows)
    ye = _experts(xs, block_exp, block_valid, bf(moe_w_gate), bf(moe_w_up), bf(moe_w_down))
    out = _combine(dest, info, x.reshape(b * s, d), g2, final_g, ye, s)
    return out.reshape(b, s, d)
```

```python
import functools
import math

import jax
import jax.numpy as jnp
from jax import lax
from jax.experimental import pallas as pl
from jax.experimental.pallas import tpu as pltpu

F32 = jnp.float32
BF16 = jnp.bfloat16
HIGHEST = lax.Precision.HIGHEST

D_MODEL = 1024
EPS = 1e-6
M_D_INNER = 2048
M_HEAD_DIM = 64
M_HEADS = 32
M_GROUPS = 8
M_HPG = 4
M_STATE = 128
M_CONV = 4
M_CHUNK = 128
M_CONV_COLS = 512
M_WIN_PAD = 16
M_GROUP_W = M_HPG * M_HEAD_DIM
M_BC_W = 2 * M_GROUPS * M_STATE
M_CONV_DIM = M_D_INNER + M_BC_W
M_MAIN_W = M_D_INNER + M_CONV_DIM
A_HEADS = 8
A_HEAD_DIM = 64
A_V_DIM = 128
A_TQ = 512
A_TK = 512
FFN_DIM = 2816
N_EXPERTS = 8
TOP_K = 2
EXPERT_DIM = 3584
MOE_ROWS = 512
LANES = 128
VMEM_LIMIT = 56 * 1024 * 1024


def _silu(v):
    return v * jax.nn.sigmoid(v)


def _norm_mod(x, sc, sh):
    ms = jnp.mean(x * x, axis=-1, keepdims=True)
    return x * lax.rsqrt(ms + EPS) * (1.0 + sc) + sh


def _cparams(sem):
    return pltpu.CompilerParams(dimension_semantics=sem, vmem_limit_bytes=VMEM_LIMIT)


def _resident(shape):
    nd = len(shape)
    return pl.BlockSpec(shape, lambda *_: (0,) * nd, pipeline_mode=pl.Buffered(1))


def _adaln_kernel(c_ref, w_ref, b_ref, o_ref):
    o_ref[...] = jnp.dot(_silu(c_ref[...]), w_ref[...], preferred_element_type=F32,
                         precision=HIGHEST) + b_ref[...]


def _adaln(c8, w, b):
    rows, d = c8.shape
    n = w.shape[1]
    tn = 1024
    return pl.pallas_call(
        _adaln_kernel,
        grid=(n // tn,),
        in_specs=[pl.BlockSpec((rows, d), lambda j: (0, 0)),
                  pl.BlockSpec((d, tn), lambda j: (0, j)),
                  pl.BlockSpec((1, tn), lambda j: (0, j))],
        out_specs=pl.BlockSpec((rows, tn), lambda j: (0, j)),
        out_shape=jax.ShapeDtypeStruct((rows, n), F32),
        compiler_params=_cparams(("arbitrary",)),
        name="adaln",
    )(c8, w, b)


def _norm_proj_kernel(x_ref, sc_ref, sh_ref, w_ref, *rest, n_chunk, has_aux):
    if has_aux:
        waux_ref, o_ref, aux_ref = rest
    else:
        (o_ref,) = rest
    h = _norm_mod(x_ref[0], sc_ref[0], sh_ref[0]).astype(BF16)
    n = w_ref.shape[1]
    for c0 in range(0, n, n_chunk):
        res = jnp.dot(h, w_ref[:, c0:c0 + n_chunk],
                      preferred_element_type=F32).astype(o_ref.dtype)
        if len(o_ref.shape) == 3:
            o_ref[0, :, c0:c0 + n_chunk] = res
        else:
            for l0 in range(0, n_chunk, LANES):
                o_ref[0, (c0 + l0) // LANES] = res[:, l0:l0 + LANES]
    if has_aux:
        aux_ref[0] = jnp.dot(h, waux_ref[...], preferred_element_type=F32)


def _norm_proj(x, sc, sh, w, w_aux=None, *, head_major=False, tm=512, n_chunk=1024):
    b, s, d = x.shape
    n = w.shape[1]
    tm = min(tm, s)
    in_specs = [pl.BlockSpec((1, tm, d), lambda bi, i: (bi, i, 0)),
                pl.BlockSpec((1, 1, d), lambda bi, i: (bi, 0, 0)),
                pl.BlockSpec((1, 1, d), lambda bi, i: (bi, 0, 0)),
                _resident((d, n))]
    if head_major:
        out_specs = [pl.BlockSpec((1, n // LANES, tm, LANES), lambda bi, i: (bi, 0, i, 0))]
        out_shape = [jax.ShapeDtypeStruct((b, n // LANES, s, LANES), BF16)]
    else:
        out_specs = [pl.BlockSpec((1, tm, n), lambda bi, i: (bi, i, 0))]
        out_shape = [jax.ShapeDtypeStruct((b, s, n), BF16)]
    args = [x, sc, sh, w]
    if w_aux is not None:
        na = w_aux.shape[1]
        in_specs.append(_resident((d, na)))
        out_specs.append(pl.BlockSpec((1, tm, na), lambda bi, i: (bi, i, 0)))
        out_shape.append(jax.ShapeDtypeStruct((b, s, na), F32))
        args.append(w_aux)
    out = pl.pallas_call(
        functools.partial(_norm_proj_kernel, n_chunk=n_chunk, has_aux=w_aux is not None),
        grid=(b, s // tm),
        in_specs=in_specs, out_specs=out_specs, out_shape=out_shape,
        compiler_params=_cparams(("parallel", "parallel")),
        name="norm_proj",
    )(*args)
    return out if w_aux is not None else out[0]


def _proj_res_kernel(a_ref, w_ref, x_ref, g_ref, o_ref):
    if len(a_ref.shape) == 3:
        a = a_ref[0]
    else:
        a = jnp.concatenate([a_ref[0, h] for h in range(a_ref.shape[1])], axis=1)
    o_ref[0] = x_ref[0] + g_ref[0] * jnp.dot(a, w_ref[...], preferred_element_type=F32)


def _proj_res(a, w, x, g, *, tm=512):
    b, s, _ = x.shape
    k, d = w.shape
    tm = min(tm, s)
    if len(a.shape) == 3:
        a_spec = pl.BlockSpec((1, tm, k), lambda bi, i: (bi, i, 0))
    else:
        a_spec = pl.BlockSpec((1, k // LANES, tm, LANES), lambda bi, i: (bi, 0, i, 0))
    return pl.pallas_call(
        _proj_res_kernel,
        grid=(b, s // tm),
        in_specs=[a_spec,
                  _resident((k, d)),
                  pl.BlockSpec((1, tm, d), lambda bi, i: (bi, i, 0)),
                  pl.BlockSpec((1, 1, d), lambda bi, i: (bi, 0, 0))],
        out_specs=pl.BlockSpec((1, tm, d), lambda bi, i: (bi, i, 0)),
        out_shape=jax.ShapeDtypeStruct((b, s, d), F32),
        compiler_params=_cparams(("parallel", "parallel")),
        name="proj_res",
    )(a, w, x, g)


def _softplus(v):
    return jnp.maximum(v, 0.0) + jnp.log1p(jnp.exp(-jnp.abs(v)))


def _ssd_kernel(z_ref, x_ref, bc_ref, dt_ref, convw_ref, convb_ref, dtb_ref, alog_ref, dskip_ref,
                ng_ref, expand_ref, shift_ref, y_ref, win_ref, xbc_ref, state_ref):
    q = M_CHUNK
    gw = M_GROUP_W

    @pl.when(pl.program_id(1) == 0)
    def _():
        win_ref[0:M_WIN_PAD, :] = jnp.zeros((M_WIN_PAD, M_CONV_DIM), BF16)
        state_ref[...] = jnp.zeros_like(state_ref)

    win_ref[M_WIN_PAD:M_WIN_PAD + q, 0:M_D_INNER] = x_ref[0]
    win_ref[M_WIN_PAD:M_WIN_PAD + q, M_D_INNER:M_CONV_DIM] = bc_ref[0]
    shift = shift_ref[...]

    def conv_block(j):
        cols = slice(j * M_CONV_COLS, (j + 1) * M_CONV_COLS)
        shifted = jnp.dot(shift, win_ref[:, cols], preferred_element_type=F32)
        acc = (convb_ref[:, cols] + convw_ref[M_CONV - 1:M_CONV, cols]
               * win_ref[M_WIN_PAD:M_WIN_PAD + q, cols].astype(F32))
        for k in range(M_CONV - 1):
            acc = acc + convw_ref[k:k + 1, cols] * shifted[k * q:(k + 1) * q, :]
        xbc_ref[:, cols] = _silu(acc)

    for j in (4, 6, 0):
        conv_block(j)
    conv_after_group = {0: 1, 1: 5, 2: 7, 3: 2, 4: 3}

    dt = _softplus(dt_ref[0] + dtb_ref[...])
    da = dt * (-jnp.exp(alog_ref[...]))
    ri = lax.broadcasted_iota(jnp.int32, (q, q), 0)
    ci = lax.broadcasted_iota(jnp.int32, (q, q), 1)
    tril = ci <= ri
    acs = jnp.dot(tril.astype(F32), da, preferred_element_type=F32, precision=HIGHEST)
    acs_t = acs.T
    dt_t = dt.T
    a_last = acs[q - 1:q, :]

    expand = expand_ref[...]

    def per_channel(v):
        hi = v.astype(BF16)
        lo = (v - hi.astype(F32)).astype(BF16)
        return (jnp.dot(hi, expand, preferred_element_type=F32)
                + jnp.dot(lo, expand, preferred_element_type=F32))

    decay_in = per_channel(jnp.exp(acs))
    decay_out_dt = per_channel(jnp.exp(a_last - acs) * dt)
    decay_chunk = decay_in[q - 1:q, :]

    lane = lax.broadcasted_iota(jnp.int32, (q, LANES), 1)
    for g in range(M_GROUPS):
        b16 = xbc_ref[:, M_D_INNER + g * M_STATE:M_D_INNER + (g + 1) * M_STATE].astype(BF16)
        c16 = xbc_ref[:, M_D_INNER + (M_GROUPS + g) * M_STATE:
                  M_D_INNER + (M_GROUPS + g + 1) * M_STATE].astype(BF16)
        xs = xbc_ref[:, g * gw:(g + 1) * gw]
        xs16 = xs.astype(BF16)
        cb = lax.dot_general(c16, b16, (((1,), (1,)), ((), ())), preferred_element_type=F32)
        state = state_ref[g]
        y_off = (jnp.dot(c16, state.astype(BF16), preferred_element_type=F32)
                 * decay_in[:, g * gw:(g + 1) * gw])
        pairs = []
        for p in range(M_HPG // 2):
            ms = []
            for r in range(2):
                h = g * M_HPG + 2 * p + r
                seg = acs[:, h:h + 1] - acs_t[h:h + 1, :]
                decay = jnp.exp(jnp.where(tril, seg, -jnp.inf))
                ms.append((cb * decay * dt_t[h:h + 1, :]).astype(BF16))
            x2 = xs16[:, p * LANES:(p + 1) * LANES]
            zero = jnp.zeros_like(x2)
            rhs = jnp.concatenate([jnp.where(lane < M_HEAD_DIM, x2, zero),
                                   jnp.where(lane >= M_HEAD_DIM, x2, zero)], axis=0)
            pairs.append(jnp.dot(jnp.concatenate(ms, axis=1), rhs, preferred_element_type=F32))
        y = jnp.concatenate(pairs, axis=1) + y_off + xs * dskip_ref[:, g * gw:(g + 1) * gw]

        xw = (xs * decay_out_dt[:, g * gw:(g + 1) * gw]).astype(BF16)
        state_ref[g] = (state * decay_chunk[:, g * gw:(g + 1) * gw]
                        + lax.dot_general(b16, xw, (((0,), (0,)), ((), ())),
                                          preferred_element_type=F32))

        y = y * _silu(z_ref[0, :, g * gw:(g + 1) * gw].astype(F32))
        ms2 = jnp.mean(y * y, axis=-1, keepdims=True)
        y_ref[0, :, g * gw:(g + 1) * gw] = (
            y * lax.rsqrt(ms2 + EPS) * ng_ref[:, g * gw:(g + 1) * gw]).astype(BF16)
        if g in conv_after_group:
            conv_block(conv_after_group[g])

    win_ref[0:M_WIN_PAD, :] = win_ref[q:q + M_WIN_PAD, :]


def _ssd(main, dt_raw, conv_w, conv_b, dt_bias, a_log, d_skip, norm_g):
    b, s, _ = main.shape
    nc = s // M_CHUNK
    pad = LANES - M_HEADS
    head = lax.broadcasted_iota(jnp.int32, (LANES, M_D_INNER), 0)
    chan = lax.broadcasted_iota(jnp.int32, (LANES, M_D_INNER), 1)
    expand = (chan // M_HEAD_DIM == head).astype(BF16)
    win_rows = M_WIN_PAD + M_CHUNK
    out_row = lax.broadcasted_iota(jnp.int32, ((M_CONV - 1) * M_CHUNK, win_rows), 0)
    win_row = lax.broadcasted_iota(jnp.int32, ((M_CONV - 1) * M_CHUNK, win_rows), 1)
    shift = (win_row == out_row % M_CHUNK + out_row // M_CHUNK + M_WIN_PAD - (M_CONV - 1)).astype(BF16)
    vec = lambda n: pl.BlockSpec((1, n), lambda bi, c: (0, 0))
    blk = lambda j: pl.BlockSpec((1, M_CHUNK, M_D_INNER), lambda bi, c: (bi, c, j))
    return pl.pallas_call(
        _ssd_kernel,
        grid=(b, nc),
        in_specs=[blk(0), blk(1), blk(2),
                  pl.BlockSpec((1, M_CHUNK, LANES), lambda bi, c: (bi, c, 0)),
                  pl.BlockSpec((M_CONV, M_CONV_DIM), lambda bi, c: (0, 0)),
                  vec(M_CONV_DIM), vec(LANES), vec(LANES), vec(M_D_INNER), vec(M_D_INNER),
                  pl.BlockSpec((LANES, M_D_INNER), lambda bi, c: (0, 0)),
                  pl.BlockSpec(shift.shape, lambda bi, c: (0, 0))],
        out_specs=pl.BlockSpec((1, M_CHUNK, M_D_INNER), lambda bi, c: (bi, c, 0)),
        out_shape=jax.ShapeDtypeStruct((b, s, M_D_INNER), BF16),
        scratch_shapes=[pltpu.VMEM((win_rows, M_CONV_DIM), BF16),
                        pltpu.VMEM((M_CHUNK, M_CONV_DIM), F32),
                        pltpu.VMEM((M_GROUPS, M_STATE, M_GROUP_W), F32)],
        compiler_params=_cparams(("parallel", "arbitrary")),
        name="ssd",
    )(main, main, main, dt_raw, conv_w, conv_b[None, :],
      jnp.pad(dt_bias, (0, pad))[None, :], jnp.pad(a_log, (0, pad))[None, :],
      jnp.repeat(d_skip, M_HEAD_DIM)[None, :], norm_g[None, :], expand, shift)


def _swiglu_kernel(x_ref, sc_ref, sh_ref, g_ref, wg_ref, wu_ref, wd_ref, o_ref, *, f_chunk):
    x = x_ref[0]
    h = _norm_mod(x, sc_ref[0], sh_ref[0]).astype(BF16)
    f = wg_ref.shape[1]
    acc = jnp.zeros(x.shape, F32)
    for c0 in range(0, f, f_chunk):
        c1 = min(c0 + f_chunk, f)
        gate = jnp.dot(h, wg_ref[:, c0:c1], preferred_element_type=F32)
        up = jnp.dot(h, wu_ref[:, c0:c1], preferred_element_type=F32)
        acc = acc + jnp.dot((_silu(gate) * up).astype(BF16), wd_ref[c0:c1, :],
                            preferred_element_type=F32)
    o_ref[0] = x + g_ref[0] * acc


def _swiglu(x, sc, sh, g, wg, wu, wd, *, tm=512, f_chunk=512):
    b, s, d = x.shape
    f = wg.shape[1]
    tm = min(tm, s)
    mod = pl.BlockSpec((1, 1, d), lambda bi, i: (bi, 0, 0))
    return pl.pallas_call(
        functools.partial(_swiglu_kernel, f_chunk=f_chunk),
        grid=(b, s // tm),
        in_specs=[pl.BlockSpec((1, tm, d), lambda bi, i: (bi, i, 0)), mod, mod, mod,
                  _resident((d, f)), _resident((d, f)), _resident((f, d))],
        out_specs=pl.BlockSpec((1, tm, d), lambda bi, i: (bi, i, 0)),
        out_shape=jax.ShapeDtypeStruct((b, s, d), F32),
        compiler_params=_cparams(("parallel", "parallel")),
        name="swiglu",
    )(x, sc, sh, g, wg, wu, wd)


A_V_ROWS = A_V_DIM + 16


def _attn_kernel(q_ref, k_ref, v_ref, lam_ref, g_ref, o_ref, vt_ref, qm_ref, st_ref, p_ref, bm_ref,
                 m_ref, al_ref, acc_ref, *, tq, tk, lambda_init):
    qi = pl.program_id(2)

    @pl.when(qi == 0)
    def _():
        for j in range(vt_ref.shape[0]):
            vt_ref[j, 0:A_V_DIM, :] = v_ref[0, 0, j * tk:(j + 1) * tk, :].astype(F32).T.astype(BF16)
            vt_ref[j, A_V_DIM:A_V_ROWS, :] = jnp.ones((A_V_ROWS - A_V_DIM, tk), BF16)

    lane = lax.broadcasted_iota(jnp.int32, (tq, LANES), 1)
    q = q_ref[0, 0]
    q = (q.astype(F32) * (A_HEAD_DIM ** -0.5)).astype(BF16)
    zero = jnp.zeros_like(q)
    qm_ref[0] = jnp.where(lane < A_HEAD_DIM, q, zero)
    qm_ref[1] = jnp.where(lane >= A_HEAD_DIM, q, zero)
    m_ref[...] = jnp.full(m_ref.shape, -jnp.inf, F32)
    acc_ref[...] = jnp.zeros(acc_ref.shape, F32)
    p_ref[...] = jnp.zeros(p_ref.shape, BF16)
    al_ref[...] = jnp.ones(al_ref.shape, F32)

    last = (qi * tq) // tk

    def column_max(st):
        top = st
        while top.shape[0] > 32:
            half = top.shape[0] // 2
            top = jnp.maximum(top[:half], top[half:])
        return jnp.max(top, axis=0, keepdims=True)

    def scores(blk, j):
        start = pl.multiple_of(blk * tk, tk)
        st = lax.dot_general(k_ref[0, 0, pl.ds(start, tk), :], qm_ref[j],
                             (((1,), (1,)), ((), ())), preferred_element_type=F32)
        st_ref[j] = st
        bm_ref[j] = column_max(st)

    def softmax(blk, j, masked):
        st = st_ref[j]
        if masked:
            key = blk * tk + lax.broadcasted_iota(jnp.int32, (tk, tq), 0)
            qry = qi * tq + lax.broadcasted_iota(jnp.int32, (tk, tq), 1)
            st = jnp.where(key <= qry, st, -jnp.inf)
            block_max = column_max(st)
        else:
            block_max = bm_ref[j]
        m_old = m_ref[j]
        m_new = jnp.maximum(m_old, block_max)
        al_ref[j] = jnp.exp(m_old - m_new)
        p_ref[j] = jnp.exp(st - m_new).astype(BF16)
        m_ref[j] = m_new

    def weighted_values(blk, j):
        acc_ref[j] = al_ref[j] * acc_ref[j] + jnp.dot(vt_ref[blk], p_ref[j],
                                                      preferred_element_type=F32)

    for j in range(2):
        scores(0, j)

    def visit(blk, diagonal):
        for j in range(2):
            weighted_values(jnp.maximum(blk - 1, 0), j)
            softmax(blk, j, diagonal)
            if not diagonal:
                scores(blk + 1, j)

    def body(pair, carry):
        visit(2 * pair, False)
        visit(2 * pair + 1, False)
        return carry

    lax.fori_loop(0, last // 2, body, 0)

    @pl.when(last % 2 == 1)
    def _():
        visit(last - 1, False)

    visit(last, True)
    for j in range(2):
        weighted_values(last, j)

    lam_v = lam_ref[...]
    lam = (jnp.exp(jnp.sum(lam_v[0:1] * lam_v[1:2], axis=-1, keepdims=True))
           - jnp.exp(jnp.sum(lam_v[2:3] * lam_v[3:4], axis=-1, keepdims=True)) + lambda_init)
    acc0 = acc_ref[0]
    acc1 = acc_ref[1]
    o = (acc0[0:A_V_DIM] / acc0[A_V_DIM:A_V_DIM + 1]
         - lam * (acc1[0:A_V_DIM] / acc1[A_V_DIM:A_V_DIM + 1])).T
    o = o * lax.rsqrt(jnp.mean(o * o, axis=-1, keepdims=True) + EPS)
    o_ref[0, 0] = (o * g_ref[...] * (1.0 - lambda_init)).astype(BF16)


def _attention(qkv, lam_pack, subln_g, lambda_init):
    b, _, s, _ = qkv.shape
    tq = min(A_TQ, s)
    tk = min(A_TK, s)
    return pl.pallas_call(
        functools.partial(_attn_kernel, tq=tq, tk=tk, lambda_init=lambda_init),
        grid=(b, A_HEADS, s // tq),
        in_specs=[pl.BlockSpec((1, 1, tq, LANES), lambda bi, h, i: (bi, h, i, 0)),
                  pl.BlockSpec((1, 1, s, LANES), lambda bi, h, i: (bi, A_HEADS + h, 0, 0)),
                  pl.BlockSpec((1, 1, s, LANES), lambda bi, h, i: (bi, 2 * A_HEADS + h, 0, 0)),
                  pl.BlockSpec((8, LANES), lambda bi, h, i: (0, 0)),
                  pl.BlockSpec((1, LANES), lambda bi, h, i: (0, 0))],
        out_specs=pl.BlockSpec((1, 1, tq, LANES), lambda bi, h, i: (bi, h, i, 0)),
        out_shape=jax.ShapeDtypeStruct((b, A_HEADS, s, A_V_DIM), BF16),
        scratch_shapes=[pltpu.VMEM((s // tk, A_V_ROWS, tk), BF16),
                        pltpu.VMEM((2, tq, LANES), BF16),
                        pltpu.VMEM((2, tk, tq), F32),
                        pltpu.VMEM((2, tk, tq), BF16),
                        pltpu.VMEM((2, 1, tq), F32),
                        pltpu.VMEM((2, 1, tq), F32),
                        pltpu.VMEM((2, 1, tq), F32),
                        pltpu.VMEM((2, A_V_ROWS, tq), F32)],
        compiler_params=_cparams(("parallel", "parallel", "arbitrary")),
        name="diff_attn",
    )(qkv, qkv, qkv, lam_pack, subln_g[None, :])


def _router_kernel(x_ref, sc_ref, sh_ref, wr_ref, h_ref, info_ref, cnt_ref, run_ref):
    tm = x_ref.shape[1]

    @pl.when((pl.program_id(0) == 0) & (pl.program_id(1) == 0))
    def _():
        run_ref[...] = jnp.zeros_like(run_ref)

    h = _norm_mod(x_ref[0], sc_ref[0], sh_ref[0])
    h_ref[...] = h
    h_hi = h.astype(BF16)
    h_lo = (h - h_hi.astype(F32)).astype(BF16)
    logits = (jnp.dot(h_hi, wr_ref[0], preferred_element_type=F32)
              + jnp.dot(h_lo, wr_ref[0], preferred_element_type=F32)
              + jnp.dot(h_hi, wr_ref[1], preferred_element_type=F32))
    lane = lax.broadcasted_iota(jnp.int32, (tm, LANES), 1)
    lg = jnp.where(lane < N_EXPERTS, logits, -jnp.inf)
    m0 = jnp.max(lg, axis=-1, keepdims=True)
    e0 = jnp.min(jnp.where(lg == m0, lane, LANES), axis=-1, keepdims=True)
    lg1 = jnp.where(lane == e0, -jnp.inf, lg)
    m1 = jnp.max(lg1, axis=-1, keepdims=True)
    e1 = jnp.min(jnp.where(lg1 == m1, lane, LANES), axis=-1, keepdims=True)
    ex = jnp.exp(m1 - m0)
    g0 = 1.0 / (1.0 + ex)
    g1 = ex / (1.0 + ex)

    pick0 = lane == e0
    pick1 = lane == e1
    onehot = (pick0 | pick1).astype(BF16)
    ri = lax.broadcasted_iota(jnp.int32, (tm, tm), 0)
    ci = lax.broadcasted_iota(jnp.int32, (tm, tm), 1)
    before = jnp.dot((ci < ri).astype(BF16), onehot, preferred_element_type=F32) + run_ref[0:1, :]
    rank0 = jnp.sum(jnp.where(pick0, before, 0.0), axis=-1, keepdims=True)
    rank1 = jnp.sum(jnp.where(pick1, before, 0.0), axis=-1, keepdims=True)
    total = run_ref[0:1, :] + jnp.sum(onehot.astype(F32), axis=0, keepdims=True)
    run_ref[...] = jnp.broadcast_to(total, run_ref.shape)
    cnt_ref[...] = jnp.broadcast_to(total, cnt_ref.shape)

    info = jnp.zeros((tm, LANES), F32)
    for i, v in enumerate((e0.astype(F32), e1.astype(F32), g0, g1, rank0, rank1)):
        info = jnp.where(lane == i, v, info)
    info_ref[...] = info


def _router(x, sc, sh, w_router, *, tm=256):
    b, s, d = x.shape
    tm = min(tm, s)
    nt = s // tm
    mod = pl.BlockSpec((1, 1, d), lambda bi, i: (bi, 0, 0))
    w = jnp.pad(w_router, ((0, 0), (0, LANES - N_EXPERTS)))
    w_hi = w.astype(BF16)
    w_split = jnp.stack([w_hi, (w - w_hi.astype(F32)).astype(BF16)])
    return pl.pallas_call(
        _router_kernel,
        grid=(b, nt),
        in_specs=[pl.BlockSpec((1, tm, d), lambda bi, i: (bi, i, 0)), mod, mod,
                  pl.BlockSpec((2, d, LANES), lambda bi, i: (0, 0, 0))],
        out_specs=[pl.BlockSpec((tm, d), lambda bi, i: (bi * nt + i, 0)),
                   pl.BlockSpec((tm, LANES), lambda bi, i: (bi * nt + i, 0)),
                   pl.BlockSpec((8, LANES), lambda bi, i: (0, 0))],
        out_shape=[jax.ShapeDtypeStruct((b * s, d), F32),
                   jax.ShapeDtypeStruct((b * s, LANES), F32),
                   jax.ShapeDtypeStruct((8, LANES), F32)],
        scratch_shapes=[pltpu.VMEM((8, LANES), F32)],
        compiler_params=_cparams(("arbitrary", "arbitrary")),
        name="moe_router",
    )(x, sc, sh, w_split)


def _row_copy(src_ref, src_row, dst_ref, dst_row, sem):
    return pltpu.make_async_copy(src_ref.at[pl.ds(src_row, 1)], dst_ref.at[pl.ds(dst_row, 1)], sem)


def _dispatch_kernel(dest_ref, pend_ref, h_ref, xs_ref, zero_ref, sem, zero_sem):
    tm = h_ref.shape[0]

    @pl.when(pl.program_id(0) == 0)
    def _():
        zero_ref[...] = jnp.zeros(zero_ref.shape, F32)

        def fill(e):
            end = pend_ref[e]
            nonempty = end > (pend_ref[e - 1] if e else 0)
            start = pl.multiple_of(jnp.maximum(end - MOE_ROWS, 0), MOE_ROWS)
            return nonempty, pltpu.make_async_copy(zero_ref, xs_ref.at[pl.ds(start, MOE_ROWS)],
                                                   zero_sem)

        for e in range(N_EXPERTS):
            nonempty, copy = fill(e)
            pl.when(nonempty)(copy.start)
        for e in range(N_EXPERTS):
            nonempty, copy = fill(e)
            pl.when(nonempty)(copy.wait)

    def issue(i, carry):
        for k in range(TOP_K):
            _row_copy(h_ref, i, xs_ref, dest_ref[TOP_K * i + k], sem).start(priority=k)
        return carry

    def drain(i, carry):
        for k in range(TOP_K):
            _row_copy(h_ref, 0, xs_ref, 0, sem).wait()
        return carry

    lax.fori_loop(0, tm, issue, 0, unroll=8)
    lax.fori_loop(0, tm, drain, 0, unroll=8)


def _dispatch(h, dest, pends, n_rows, *, tm=256):
    t, d = h.shape
    tm = min(tm, t)
    return pl.pallas_call(
        _dispatch_kernel,
        grid=(t // tm,),
        in_specs=[pl.BlockSpec((TOP_K * tm,), lambda i: (i,), memory_space=pltpu.SMEM),
                  pl.BlockSpec(memory_space=pltpu.SMEM),
                  pl.BlockSpec((tm, d), lambda i: (i, 0))],
        out_specs=pl.BlockSpec(memory_space=pl.ANY),
        out_shape=jax.ShapeDtypeStruct((n_rows, d), F32),
        scratch_shapes=[pltpu.VMEM((MOE_ROWS, d), F32), pltpu.SemaphoreType.DMA,
                        pltpu.SemaphoreType.DMA],
        compiler_params=_cparams(("arbitrary",)),
        name="moe_dispatch",
    )(dest, pends, h)


def _expert_kernel(be_ref, bv_ref, xs_ref, wg_ref, wu_ref, wd_ref, y_ref, *, f_chunk):
    del be_ref
    i = pl.program_id(0)

    @pl.when(bv_ref[i] != 0)
    def _():
        x = xs_ref[...].astype(BF16)
        f = wg_ref.shape[2]
        acc = jnp.zeros(y_ref.shape, F32)
        for c0 in range(0, f, f_chunk):
            gate = jnp.dot(x, wg_ref[0, :, c0:c0 + f_chunk], preferred_element_type=F32)
            up = jnp.dot(x, wu_ref[0, :, c0:c0 + f_chunk], preferred_element_type=F32)
            acc = acc + jnp.dot((_silu(gate) * up).astype(BF16), wd_ref[0, c0:c0 + f_chunk, :],
                                preferred_element_type=F32)
        y_ref[...] = acc

    @pl.when(bv_ref[i] == 0)
    def _():
        y_ref[...] = jnp.zeros_like(y_ref)


def _experts(xs, block_exp, block_valid, wg, wu, wd, *, f_chunk=512):
    n_rows, d = xs.shape
    f = wg.shape[2]
    n_blocks = n_rows // MOE_ROWS
    wspec = lambda shape: pl.BlockSpec(shape, lambda i, be, bv: (be[i], 0, 0),
                                       pipeline_mode=pl.Buffered(1))
    return pl.pallas_call(
        functools.partial(_expert_kernel, f_chunk=f_chunk),
        grid_spec=pltpu.PrefetchScalarGridSpec(
            num_scalar_prefetch=2,
            grid=(n_blocks,),
            in_specs=[pl.BlockSpec((MOE_ROWS, d), lambda i, be, bv: (i, 0)),
                      wspec((1, d, f)), wspec((1, d, f)), wspec((1, f, d))],
            out_specs=pl.BlockSpec((MOE_ROWS, d), lambda i, be, bv: (i, 0))),
        out_shape=jax.ShapeDtypeStruct((n_rows, d), F32),
        compiler_params=_cparams(("arbitrary",)),
        name="moe_experts",
    )(block_exp, block_valid, xs, wg, wu, wd)


def _combine_kernel(dest_ref, next_dest_ref, info_ref, x_ref, g_ref, fg_ref, y_ref, o_ref, buf_ref,
                    sem):
    tm = x_ref.shape[0]
    step = pl.program_id(0)
    slot = step % 2

    def gather(idx_ref, to_slot):
        def issue(i, carry):
            for k in range(TOP_K):
                _row_copy(y_ref, idx_ref[TOP_K * i + k], buf_ref.at[to_slot, k], i,
                          sem.at[to_slot]).start(priority=k)
            return carry
        lax.fori_loop(0, tm, issue, 0, unroll=8)

    @pl.when(step == 0)
    def _():
        gather(dest_ref, 0)

    @pl.when(step + 1 < pl.num_programs(0))
    def _():
        gather(next_dest_ref, 1 - slot)

    def drain(i, carry):
        for k in range(TOP_K):
            _row_copy(y_ref, 0, buf_ref.at[slot, k], 0, sem.at[slot]).wait()
        return carry

    lax.fori_loop(0, tm, drain, 0, unroll=8)
    info = info_ref[...]
    ffn = info[:, 2:3] * buf_ref[slot, 0] + info[:, 3:4] * buf_ref[slot, 1]
    x = x_ref[...] + g_ref[0] * ffn
    o_ref[...] = x * lax.rsqrt(jnp.mean(x * x, axis=-1, keepdims=True) + EPS) * fg_ref[...]


def _combine(dest, info, x2d, g, final_g, y, seq, *, tm=256):
    t, d = x2d.shape
    tm = min(tm, seq)
    per_seq = seq // tm
    n_tiles = t // tm
    return pl.pallas_call(
        _combine_kernel,
        grid=(n_tiles,),
        in_specs=[pl.BlockSpec((TOP_K * tm,), lambda i: (i,), memory_space=pltpu.SMEM),
                  pl.BlockSpec((TOP_K * tm,), lambda i: (jnp.minimum(i + 1, n_tiles - 1),),
                               memory_space=pltpu.SMEM),
                  pl.BlockSpec((tm, LANES), lambda i: (i, 0)),
                  pl.BlockSpec((tm, d), lambda i: (i, 0)),
                  pl.BlockSpec((1, 1, d), lambda i: (i // per_seq, 0, 0)),
                  pl.BlockSpec((1, d), lambda i: (0, 0)),
                  pl.BlockSpec(memory_space=pl.ANY)],
        out_specs=pl.BlockSpec((tm, d), lambda i: (i, 0)),
        out_shape=jax.ShapeDtypeStruct((t, d), F32),
        scratch_shapes=[pltpu.VMEM((2, TOP_K, tm, d), F32), pltpu.SemaphoreType.DMA((2,))],
        compiler_params=_cparams(("arbitrary",)),
        name="moe_combine",
    )(dest, dest, info, x2d, g, final_g[None, :], y)


def _moe_plan(info, counts):
    e = info[:, 0:TOP_K].astype(jnp.int32)
    rank = info[:, 4:4 + TOP_K].astype(jnp.int32)
    cnt = counts[0, :N_EXPERTS].astype(jnp.int32)
    padded = (cnt + MOE_ROWS - 1) // MOE_ROWS * MOE_ROWS
    pends = jnp.cumsum(padded)
    pstarts = pends - padded
    start_of = jnp.zeros_like(e)
    for j in range(N_EXPERTS):
        start_of = jnp.where(e == j, pstarts[j], start_of)
    dest = (start_of + rank).reshape(-1)
    n_blocks = (TOP_K * info.shape[0]) // MOE_ROWS + N_EXPERTS
    block_start = jnp.arange(n_blocks, dtype=jnp.int32) * MOE_ROWS
    valid = block_start < pends[-1]
    last_start = jnp.maximum(pends[-1] - MOE_ROWS, 0)
    start = jnp.where(valid, block_start, last_start)
    block_exp = jnp.sum((start[:, None] >= pends[None, :]).astype(jnp.int32), axis=1)
    block_exp = jnp.minimum(block_exp, N_EXPERTS - 1)
    return dest, pends, block_exp, valid.astype(jnp.int32), n_blocks * MOE_ROWS


def kernel(x, c, ada_w0, ada_b0, m_w_in, m_conv_w, m_conv_b, m_dt_bias, m_a_log, m_d_skip, m_norm_g, m_w_out, ffn_w_gate, ffn_w_up, ffn_w_down, ada_w1, ada_b1, a_w_qkv, a_lam_q1, a_lam_k1, a_lam_q2, a_lam_k2, a_subln_g, a_w_o, moe_w_router, moe_w_gate, moe_w_up, moe_w_down, final_g):
    b, s, d = x.shape
    bf = lambda w: w.astype(BF16)
    c8 = jnp.pad(c, ((0, 8 - b), (0, 0)))

    def mods(w, bias):
        mod = _adaln(c8, w, bias[None, :])[:b]
        return [m[:, None, :] for m in jnp.split(mod, 6, axis=-1)]

    sh1, sc1, g1, sh2, sc2, g2 = mods(ada_w0, ada_b0)
    w_dt = jnp.pad(m_w_in[:, M_MAIN_W:], ((0, 0), (0, LANES - M_HEADS)))
    main, dt_raw = _norm_proj(x, sc1, sh1, bf(m_w_in[:, :M_MAIN_W]), bf(w_dt))
    y = _ssd(main, dt_raw, m_conv_w, m_conv_b, m_dt_bias, m_a_log, m_d_skip, m_norm_g)
    x = _proj_res(y, bf(m_w_out), x, g1)
    x = _swiglu(x, sc2, sh2, g2, bf(ffn_w_gate), bf(ffn_w_up), bf(ffn_w_down))

    sh1, sc1, g1, sh2, sc2, g2 = mods(ada_w1, ada_b1)
    lambda_init = 0.8 - 0.6 * math.exp(-0.3 * 1)
    qkv = _norm_proj(x, sc1, sh1, bf(a_w_qkv), head_major=True)
    lam_pack = jnp.pad(jnp.stack([a_lam_q1, a_lam_k1, a_lam_q2, a_lam_k2]),
                       ((0, 4), (0, LANES - A_HEAD_DIM)))
    att = _attention(qkv, lam_pack, a_subln_g, lambda_init)
    x = _proj_res(att, bf(a_w_o), x, g1)

    h, info, counts = _router(x, sc2, sh2, moe_w_router)
    dest, pends, block_exp, block_valid, n_rows = _moe_plan(info, counts)
    xs = _dispatch(h, dest, pends, n_rows)
    ye = _experts(xs, block_exp, block_valid, bf(moe_w_gate), bf(moe_w_up), bf(moe_w_down))
    out = _combine(dest, info, x.reshape(b * s, d), g2, final_g, ye, s)
    return out.reshape(b, s, d)
```

```python
import functools
import math

import jax
import jax.numpy as jnp
from jax import lax
from jax.experimental import pallas as pl
from jax.experimental.pallas import tpu as pltpu

F32 = jnp.float32
BF16 = jnp.bfloat16
HIGHEST = lax.Precision.HIGHEST

D_MODEL = 1024
EPS = 1e-6
M_D_INNER = 2048
M_HEAD_DIM = 64
M_HEADS = 32
M_GROUPS = 8
M_HPG = 4
M_STATE = 128
M_CONV = 4
M_CHUNK = 128
M_CONV_COLS = 512
M_WIN_PAD = 16
M_GROUP_W = M_HPG * M_HEAD_DIM
M_BC_W = 2 * M_GROUPS * M_STATE
M_CONV_DIM = M_D_INNER + M_BC_W
M_MAIN_W = M_D_INNER + M_CONV_DIM
A_HEADS = 8
A_HEAD_DIM = 64
A_V_DIM = 128
A_TQ = 512
A_TK = 512
FFN_DIM = 2816
N_EXPERTS = 8
TOP_K = 2
EXPERT_DIM = 3584
MOE_ROWS = 512
LANES = 128
VMEM_LIMIT = 56 * 1024 * 1024


def _silu(v):
    return v * jax.nn.sigmoid(v)


def _norm_mod(x, sc, sh):
    ms = jnp.mean(x * x, axis=-1, keepdims=True)
    return x * lax.rsqrt(ms + EPS) * (1.0 + sc) + sh


def _cparams(sem):
    return pltpu.CompilerParams(dimension_semantics=sem, vmem_limit_bytes=VMEM_LIMIT)


def _resident(shape):
    nd = len(shape)
    return pl.BlockSpec(shape, lambda *_: (0,) * nd, pipeline_mode=pl.Buffered(1))


def _adaln_kernel(c_ref, w_ref, b_ref, o_ref):
    o_ref[...] = jnp.dot(_silu(c_ref[...]), w_ref[...], preferred_element_type=F32,
                         precision=HIGHEST) + b_ref[...]


def _adaln(c8, w, b):
    rows, d = c8.shape
    n = w.shape[1]
    tn = 1024
    return pl.pallas_call(
        _adaln_kernel,
        grid=(n // tn,),
        in_specs=[pl.BlockSpec((rows, d), lambda j: (0, 0)),
                  pl.BlockSpec((d, tn), lambda j: (0, j)),
                  pl.BlockSpec((1, tn), lambda j: (0, j))],
        out_specs=pl.BlockSpec((rows, tn), lambda j: (0, j)),
        out_shape=jax.ShapeDtypeStruct((rows, n), F32),
        compiler_params=_cparams(("arbitrary",)),
        name="adaln",
    )(c8, w, b)


def _norm_proj_kernel(x_ref, sc_ref, sh_ref, w_ref, *rest, n_chunk, has_aux):
    if has_aux:
        waux_ref, o_ref, aux_ref = rest
    else:
        (o_ref,) = rest
    h = _norm_mod(x_ref[0], sc_ref[0], sh_ref[0]).astype(BF16)
    n = w_ref.shape[1]
    for c0 in range(0, n, n_chunk):
        res = jnp.dot(h, w_ref[:, c0:c0 + n_chunk],
                      preferred_element_type=F32).astype(o_ref.dtype)
        if len(o_ref.shape) == 3:
            o_ref[0, :, c0:c0 + n_chunk] = res
        else:
            for l0 in range(0, n_chunk, LANES):
                o_ref[0, (c0 + l0) // LANES] = res[:, l0:l0 + LANES]
    if has_aux:
        aux_ref[0] = jnp.dot(h, waux_ref[...], preferred_element_type=F32)


def _norm_proj(x, sc, sh, w, w_aux=None, *, head_major=False, tm=512, n_chunk=1024):
    b, s, d = x.shape
    n = w.shape[1]
    tm = min(tm, s)
    in_specs = [pl.BlockSpec((1, tm, d), lambda bi, i: (bi, i, 0)),
                pl.BlockSpec((1, 1, d), lambda bi, i: (bi, 0, 0)),
                pl.BlockSpec((1, 1, d), lambda bi, i: (bi, 0, 0)),
                _resident((d, n))]
    if head_major:
        out_specs = [pl.BlockSpec((1, n // LANES, tm, LANES), lambda bi, i: (bi, 0, i, 0))]
        out_shape = [jax.ShapeDtypeStruct((b, n // LANES, s, LANES), BF16)]
    else:
        out_specs = [pl.BlockSpec((1, tm, n), lambda bi, i: (bi, i, 0))]
        out_shape = [jax.ShapeDtypeStruct((b, s, n), BF16)]
    args = [x, sc, sh, w]
    if w_aux is not None:
        na = w_aux.shape[1]
        in_specs.append(_resident((d, na)))
        out_specs.append(pl.BlockSpec((1, tm, na), lambda bi, i: (bi, i, 0)))
        out_shape.append(jax.ShapeDtypeStruct((b, s, na), F32))
        args.append(w_aux)
    out = pl.pallas_call(
        functools.partial(_norm_proj_kernel, n_chunk=n_chunk, has_aux=w_aux is not None),
        grid=(b, s // tm),
        in_specs=in_specs, out_specs=out_specs, out_shape=out_shape,
        compiler_params=_cparams(("parallel", "parallel")),
        name="norm_proj",
    )(*args)
    return out if w_aux is not None else out[0]


def _proj_res_kernel(a_ref, w_ref, x_ref, g_ref, o_ref):
    if len(a_ref.shape) == 3:
        a = a_ref[0]
    else:
        a = jnp.concatenate([a_ref[0, h] for h in range(a_ref.shape[1])], axis=1)
    o_ref[0] = x_ref[0] + g_ref[0] * jnp.dot(a, w_ref[...], preferred_element_type=F32)


def _proj_res(a, w, x, g, *, tm=1024):
    b, s, _ = x.shape
    k, d = w.shape
    tm = min(tm, s)
    if len(a.shape) == 3:
        a_spec = pl.BlockSpec((1, tm, k), lambda bi, i: (bi, i, 0))
    else:
        a_spec = pl.BlockSpec((1, k // LANES, tm, LANES), lambda bi, i: (bi, 0, i, 0))
    return pl.pallas_call(
        _proj_res_kernel,
        grid=(b, s // tm),
        in_specs=[a_spec,
                  _resident((k, d)),
                  pl.BlockSpec((1, tm, d), lambda bi, i: (bi, i, 0)),
                  pl.BlockSpec((1, 1, d), lambda bi, i: (bi, 0, 0))],
        out_specs=pl.BlockSpec((1, tm, d), lambda bi, i: (bi, i, 0)),
        out_shape=jax.ShapeDtypeStruct((b, s, d), F32),
        compiler_params=_cparams(("parallel", "parallel")),
        name="proj_res",
    )(a, w, x, g)


def _softplus(v):
    return jnp.maximum(v, 0.0) + jnp.log1p(jnp.exp(-jnp.abs(v)))


def _ssd_kernel(z_ref, x_ref, bc_ref, dt_ref, convw_ref, convb_ref, dtb_ref, alog_ref, dskip_ref,
                ng_ref, expand_ref, shift_ref, y_ref, win_ref, xbc_ref, state_ref):
    q = M_CHUNK
    gw = M_GROUP_W

    @pl.when(pl.program_id(1) == 0)
    def _():
        win_ref[0:M_WIN_PAD, :] = jnp.zeros((M_WIN_PAD, M_CONV_DIM), BF16)
        state_ref[...] = jnp.zeros_like(state_ref)

    win_ref[M_WIN_PAD:M_WIN_PAD + q, 0:M_D_INNER] = x_ref[0]
    win_ref[M_WIN_PAD:M_WIN_PAD + q, M_D_INNER:M_CONV_DIM] = bc_ref[0]
    shift = shift_ref[...]

    def conv_block(j):
        cols = slice(j * M_CONV_COLS, (j + 1) * M_CONV_COLS)
        shifted = jnp.dot(shift, win_ref[:, cols], preferred_element_type=F32)
        acc = (convb_ref[:, cols] + convw_ref[M_CONV - 1:M_CONV, cols]
               * win_ref[M_WIN_PAD:M_WIN_PAD + q, cols].astype(F32))
        for k in range(M_CONV - 1):
            acc = acc + convw_ref[k:k + 1, cols] * shifted[k * q:(k + 1) * q, :]
        xbc_ref[:, cols] = _silu(acc)

    for j in (4, 6, 0):
        conv_block(j)
    conv_after_group = {0: 1, 1: 5, 2: 7, 3: 2, 4: 3}

    dt = _softplus(dt_ref[0] + dtb_ref[...])
    da = dt * (-jnp.exp(alog_ref[...]))
    ri = lax.broadcasted_iota(jnp.int32, (q, q), 0)
    ci = lax.broadcasted_iota(jnp.int32, (q, q), 1)
    tril = ci <= ri
    acs = jnp.dot(tril.astype(F32), da, preferred_element_type=F32, precision=HIGHEST)
    acs_t = acs.T
    dt_t = dt.T
    a_last = acs[q - 1:q, :]

    expand = expand_ref[...]

    def per_channel(v):
        hi = v.astype(BF16)
        lo = (v - hi.astype(F32)).astype(BF16)
        return (jnp.dot(hi, expand, preferred_element_type=F32)
                + jnp.dot(lo, expand, preferred_element_type=F32))

    decay_in = per_channel(jnp.exp(acs))
    decay_out_dt = per_channel(jnp.exp(a_last - acs) * dt)
    decay_chunk = decay_in[q - 1:q, :]

    lane = lax.broadcasted_iota(jnp.int32, (q, LANES), 1)
    for g in range(M_GROUPS):
        b16 = xbc_ref[:, M_D_INNER + g * M_STATE:M_D_INNER + (g + 1) * M_STATE].astype(BF16)
        c16 = xbc_ref[:, M_D_INNER + (M_GROUPS + g) * M_STATE:
                  M_D_INNER + (M_GROUPS + g + 1) * M_STATE].astype(BF16)
        xs = xbc_ref[:, g * gw:(g + 1) * gw]
        xs16 = xs.astype(BF16)
        cb = lax.dot_general(c16, b16, (((1,), (1,)), ((), ())), preferred_element_type=F32)
        state = state_ref[g]
        y_off = (jnp.dot(c16, state.astype(BF16), preferred_element_type=F32)
                 * decay_in[:, g * gw:(g + 1) * gw])
        pairs = []
        for p in range(M_HPG // 2):
            ms = []
            for r in range(2):
                h = g * M_HPG + 2 * p + r
                seg = acs[:, h:h + 1] - acs_t[h:h + 1, :]
                decay = jnp.exp(jnp.where(tril, seg, -jnp.inf))
                ms.append((cb * decay * dt_t[h:h + 1, :]).astype(BF16))
            x2 = xs16[:, p * LANES:(p + 1) * LANES]
            zero = jnp.zeros_like(x2)
            rhs = jnp.concatenate([jnp.where(lane < M_HEAD_DIM, x2, zero),
                                   jnp.where(lane >= M_HEAD_DIM, x2, zero)], axis=0)
            pairs.append(jnp.dot(jnp.concatenate(ms, axis=1), rhs, preferred_element_type=F32))
        y = jnp.concatenate(pairs, axis=1) + y_off + xs * dskip_ref[:, g * gw:(g + 1) * gw]

        xw = (xs * decay_out_dt[:, g * gw:(g + 1) * gw]).astype(BF16)
        state_ref[g] = (state * decay_chunk[:, g * gw:(g + 1) * gw]
                        + lax.dot_general(b16, xw, (((0,), (0,)), ((), ())),
                                          preferred_element_type=F32))

        y = y * _silu(z_ref[0, :, g * gw:(g + 1) * gw].astype(F32))
        ms2 = jnp.mean(y * y, axis=-1, keepdims=True)
        y_ref[0, :, g * gw:(g + 1) * gw] = (
            y * lax.rsqrt(ms2 + EPS) * ng_ref[:, g * gw:(g + 1) * gw]).astype(BF16)
        if g in conv_after_group:
            conv_block(conv_after_group[g])

    win_ref[0:M_WIN_PAD, :] = win_ref[q:q + M_WIN_PAD, :]


def _ssd(main, dt_raw, conv_w, conv_b, dt_bias, a_log, d_skip, norm_g):
    b, s, _ = main.shape
    nc = s // M_CHUNK
    pad = LANES - M_HEADS
    head = lax.broadcasted_iota(jnp.int32, (LANES, M_D_INNER), 0)
    chan = lax.broadcasted_iota(jnp.int32, (LANES, M_D_INNER), 1)
    expand = (chan // M_HEAD_DIM == head).astype(BF16)
    win_rows = M_WIN_PAD + M_CHUNK
    out_row = lax.broadcasted_iota(jnp.int32, ((M_CONV - 1) * M_CHUNK, win_rows), 0)
    win_row = lax.broadcasted_iota(jnp.int32, ((M_CONV - 1) * M_CHUNK, win_rows), 1)
    shift = (win_row == out_row % M_CHUNK + out_row // M_CHUNK + M_WIN_PAD - (M_CONV - 1)).astype(BF16)
    vec = lambda n: pl.BlockSpec((1, n), lambda bi, c: (0, 0))
    blk = lambda j: pl.BlockSpec((1, M_CHUNK, M_D_INNER), lambda bi, c: (bi, c, j))
    return pl.pallas_call(
        _ssd_kernel,
        grid=(b, nc),
        in_specs=[blk(0), blk(1), blk(2),
                  pl.BlockSpec((1, M_CHUNK, LANES), lambda bi, c: (bi, c, 0)),
                  pl.BlockSpec((M_CONV, M_CONV_DIM), lambda bi, c: (0, 0)),
                  vec(M_CONV_DIM), vec(LANES), vec(LANES), vec(M_D_INNER), vec(M_D_INNER),
                  pl.BlockSpec((LANES, M_D_INNER), lambda bi, c: (0, 0)),
                  pl.BlockSpec(shift.shape, lambda bi, c: (0, 0))],
        out_specs=pl.BlockSpec((1, M_CHUNK, M_D_INNER), lambda bi, c: (bi, c, 0)),
        out_shape=jax.ShapeDtypeStruct((b, s, M_D_INNER), BF16),
        scratch_shapes=[pltpu.VMEM((win_rows, M_CONV_DIM), BF16),
                        pltpu.VMEM((M_CHUNK, M_CONV_DIM), F32),
                        pltpu.VMEM((M_GROUPS, M_STATE, M_GROUP_W), F32)],
        compiler_params=_cparams(("parallel", "arbitrary")),
        name="ssd",
    )(main, main, main, dt_raw, conv_w, conv_b[None, :],
      jnp.pad(dt_bias, (0, pad))[None, :], jnp.pad(a_log, (0, pad))[None, :],
      jnp.repeat(d_skip, M_HEAD_DIM)[None, :], norm_g[None, :], expand, shift)


def _swiglu_kernel(x_ref, sc_ref, sh_ref, g_ref, wg_ref, wu_ref, wd_ref, o_ref, *, f_chunk):
    x = x_ref[0]
    h = _norm_mod(x, sc_ref[0], sh_ref[0]).astype(BF16)
    f = wg_ref.shape[1]
    acc = jnp.zeros(x.shape, F32)
    for c0 in range(0, f, f_chunk):
        c1 = min(c0 + f_chunk, f)
        gate = jnp.dot(h, wg_ref[:, c0:c1], preferred_element_type=F32)
        up = jnp.dot(h, wu_ref[:, c0:c1], preferred_element_type=F32)
        acc = acc + jnp.dot((_silu(gate) * up).astype(BF16), wd_ref[c0:c1, :],
                            preferred_element_type=F32)
    o_ref[0] = x + g_ref[0] * acc


def _swiglu(x, sc, sh, g, wg, wu, wd, *, tm=512, f_chunk=512):
    b, s, d = x.shape
    f = wg.shape[1]
    tm = min(tm, s)
    mod = pl.BlockSpec((1, 1, d), lambda bi, i: (bi, 0, 0))
    return pl.pallas_call(
        functools.partial(_swiglu_kernel, f_chunk=f_chunk),
        grid=(b, s // tm),
        in_specs=[pl.BlockSpec((1, tm, d), lambda bi, i: (bi, i, 0)), mod, mod, mod,
                  _resident((d, f)), _resident((d, f)), _resident((f, d))],
        out_specs=pl.BlockSpec((1, tm, d), lambda bi, i: (bi, i, 0)),
        out_shape=jax.ShapeDtypeStruct((b, s, d), F32),
        compiler_params=_cparams(("parallel", "parallel")),
        name="swiglu",
    )(x, sc, sh, g, wg, wu, wd)


A_V_ROWS = A_V_DIM + 16


def _attn_kernel(q_ref, k_ref, v_ref, lam_ref, g_ref, o_ref, vt_ref, qm_ref, st_ref, p_ref, bm_ref,
                 m_ref, al_ref, acc_ref, *, tq, tk, lambda_init):
    qi = pl.program_id(2)

    @pl.when(qi == 0)
    def _():
        for j in range(vt_ref.shape[0]):
            vt_ref[j, 0:A_V_DIM, :] = v_ref[0, 0, j * tk:(j + 1) * tk, :].astype(F32).T.astype(BF16)
            vt_ref[j, A_V_DIM:A_V_ROWS, :] = jnp.ones((A_V_ROWS - A_V_DIM, tk), BF16)

    lane = lax.broadcasted_iota(jnp.int32, (tq, LANES), 1)
    q = q_ref[0, 0]
    q = (q.astype(F32) * (A_HEAD_DIM ** -0.5)).astype(BF16)
    zero = jnp.zeros_like(q)
    qm_ref[0] = jnp.where(lane < A_HEAD_DIM, q, zero)
    qm_ref[1] = jnp.where(lane >= A_HEAD_DIM, q, zero)
    m_ref[...] = jnp.full(m_ref.shape, -jnp.inf, F32)
    acc_ref[...] = jnp.zeros(acc_ref.shape, F32)
    p_ref[...] = jnp.zeros(p_ref.shape, BF16)
    al_ref[...] = jnp.ones(al_ref.shape, F32)

    last = (qi * tq) // tk

    def column_max(st):
        top = st
        while top.shape[0] > 32:
            half = top.shape[0] // 2
            top = jnp.maximum(top[:half], top[half:])
        return jnp.max(top, axis=0, keepdims=True)

    def scores(blk, j):
        start = pl.multiple_of(blk * tk, tk)
        st = lax.dot_general(k_ref[0, 0, pl.ds(start, tk), :], qm_ref[j],
                             (((1,), (1,)), ((), ())), preferred_element_type=F32)
        st_ref[j] = st
        bm_ref[j] = column_max(st)

    def softmax(blk, j, masked):
        st = st_ref[j]
        if masked:
            key = blk * tk + lax.broadcasted_iota(jnp.int32, (tk, tq), 0)
            qry = qi * tq + lax.broadcasted_iota(jnp.int32, (tk, tq), 1)
            st = jnp.where(key <= qry, st, -jnp.inf)
            block_max = column_max(st)
        else:
            block_max = bm_ref[j]
        m_old = m_ref[j]
        m_new = jnp.maximum(m_old, block_max)
        al_ref[j] = jnp.exp(m_old - m_new)
        p_ref[j] = jnp.exp(st - m_new).astype(BF16)
        m_ref[j] = m_new

    def weighted_values(blk, j):
        acc_ref[j] = al_ref[j] * acc_ref[j] + jnp.dot(vt_ref[blk], p_ref[j],
                                                      preferred_element_type=F32)

    for j in range(2):
        scores(0, j)

    def visit(blk, diagonal):
        for j in range(2):
            weighted_values(jnp.maximum(blk - 1, 0), j)
            softmax(blk, j, diagonal)
            if not diagonal:
                scores(blk + 1, j)

    def body(pair, carry):
        visit(2 * pair, False)
        visit(2 * pair + 1, False)
        return carry

    lax.fori_loop(0, last // 2, body, 0)

    @pl.when(last % 2 == 1)
    def _():
        visit(last - 1, False)

    visit(last, True)
    for j in range(2):
        weighted_values(last, j)

    lam_v = lam_ref[...]
    lam = (jnp.exp(jnp.sum(lam_v[0:1] * lam_v[1:2], axis=-1, keepdims=True))
           - jnp.exp(jnp.sum(lam_v[2:3] * lam_v[3:4], axis=-1, keepdims=True)) + lambda_init)
    acc0 = acc_ref[0]
    acc1 = acc_ref[1]
    o = (acc0[0:A_V_DIM] / acc0[A_V_DIM:A_V_DIM + 1]
         - lam * (acc1[0:A_V_DIM] / acc1[A_V_DIM:A_V_DIM + 1])).T
    o = o * lax.rsqrt(jnp.mean(o * o, axis=-1, keepdims=True) + EPS)
    o_ref[0, 0] = (o * g_ref[...] * (1.0 - lambda_init)).astype(BF16)


def _attention(qkv, lam_pack, subln_g, lambda_init):
    b, _, s, _ = qkv.shape
    tq = min(A_TQ, s)
    tk = min(A_TK, s)
    return pl.pallas_call(
        functools.partial(_attn_kernel, tq=tq, tk=tk, lambda_init=lambda_init),
        grid=(b, A_HEADS, s // tq),
        in_specs=[pl.BlockSpec((1, 1, tq, LANES), lambda bi, h, i: (bi, h, i, 0)),
                  pl.BlockSpec((1, 1, s, LANES), lambda bi, h, i: (bi, A_HEADS + h, 0, 0)),
                  pl.BlockSpec((1, 1, s, LANES), lambda bi, h, i: (bi, 2 * A_HEADS + h, 0, 0)),
                  pl.BlockSpec((8, LANES), lambda bi, h, i: (0, 0)),
                  pl.BlockSpec((1, LANES), lambda bi, h, i: (0, 0))],
        out_specs=pl.BlockSpec((1, 1, tq, LANES), lambda bi, h, i: (bi, h, i, 0)),
        out_shape=jax.ShapeDtypeStruct((b, A_HEADS, s, A_V_DIM), BF16),
        scratch_shapes=[pltpu.VMEM((s // tk, A_V_ROWS, tk), BF16),
                        pltpu.VMEM((2, tq, LANES), BF16),
                        pltpu.VMEM((2, tk, tq), F32),
                        pltpu.VMEM((2, tk, tq), BF16),
                        pltpu.VMEM((2, 1, tq), F32),
                        pltpu.VMEM((2, 1, tq), F32),
                        pltpu.VMEM((2, 1, tq), F32),
                        pltpu.VMEM((2, A_V_ROWS, tq), F32)],
        compiler_params=_cparams(("parallel", "parallel", "arbitrary")),
        name="diff_attn",
    )(qkv, qkv, qkv, lam_pack, subln_g[None, :])


def _router_kernel(x_ref, sc_ref, sh_ref, wr_ref, h_ref, info_ref, cnt_ref, run_ref):
    tm = x_ref.shape[1]

    @pl.when((pl.program_id(0) == 0) & (pl.program_id(1) == 0))
    def _():
        run_ref[...] = jnp.zeros_like(run_ref)

    h = _norm_mod(x_ref[0], sc_ref[0], sh_ref[0])
    h_ref[...] = h
    h_hi = h.astype(BF16)
    h_lo = (h - h_hi.astype(F32)).astype(BF16)
    logits = (jnp.dot(h_hi, wr_ref[0], preferred_element_type=F32)
              + jnp.dot(h_lo, wr_ref[0], preferred_element_type=F32)
              + jnp.dot(h_hi, wr_ref[1], preferred_element_type=F32))
    lane = lax.broadcasted_iota(jnp.int32, (tm, LANES), 1)
    lg = jnp.where(lane < N_EXPERTS, logits, -jnp.inf)
    m0 = jnp.max(lg, axis=-1, keepdims=True)
    e0 = jnp.min(jnp.where(lg == m0, lane, LANES), axis=-1, keepdims=True)
    lg1 = jnp.where(lane == e0, -jnp.inf, lg)
    m1 = jnp.max(lg1, axis=-1, keepdims=True)
    e1 = jnp.min(jnp.where(lg1 == m1, lane, LANES), axis=-1, keepdims=True)
    ex = jnp.exp(m1 - m0)
    g0 = 1.0 / (1.0 + ex)
    g1 = ex / (1.0 + ex)

    pick0 = lane == e0
    pick1 = lane == e1
    onehot = (pick0 | pick1).astype(BF16)
    ri = lax.broadcasted_iota(jnp.int32, (tm, tm), 0)
    ci = lax.broadcasted_iota(jnp.int32, (tm, tm), 1)
    before = jnp.dot((ci < ri).astype(BF16), onehot, preferred_element_type=F32) + run_ref[0:1, :]
    rank0 = jnp.sum(jnp.where(pick0, before, 0.0), axis=-1, keepdims=True)
    rank1 = jnp.sum(jnp.where(pick1, before, 0.0), axis=-1, keepdims=True)
    total = run_ref[0:1, :] + jnp.sum(onehot.astype(F32), axis=0, keepdims=True)
    run_ref[...] = jnp.broadcast_to(total, run_ref.shape)
    cnt_ref[...] = jnp.broadcast_to(total, cnt_ref.shape)

    info = jnp.zeros((tm, LANES), F32)
    for i, v in enumerate((e0.astype(F32), e1.astype(F32), g0, g1, rank0, rank1)):
        info = jnp.where(lane == i, v, info)
    info_ref[...] = info


def _router(x, sc, sh, w_router, *, tm=512):
    b, s, d = x.shape
    tm = min(tm, s)
    nt = s // tm
    mod = pl.BlockSpec((1, 1, d), lambda bi, i: (bi, 0, 0))
    w = jnp.pad(w_router, ((0, 0), (0, LANES - N_EXPERTS)))
    w_hi = w.astype(BF16)
    w_split = jnp.stack([w_hi, (w - w_hi.astype(F32)).astype(BF16)])
    return pl.pallas_call(
        _router_kernel,
        grid=(b, nt),
        in_specs=[pl.BlockSpec((1, tm, d), lambda bi, i: (bi, i, 0)), mod, mod,
                  pl.BlockSpec((2, d, LANES), lambda bi, i: (0, 0, 0))],
        out_specs=[pl.BlockSpec((tm, d), lambda bi, i: (bi * nt + i, 0)),
                   pl.BlockSpec((tm, LANES), lambda bi, i: (bi * nt + i, 0)),
                   pl.BlockSpec((8, LANES), lambda bi, i: (0, 0))],
        out_shape=[jax.ShapeDtypeStruct((b * s, d), F32),
                   jax.ShapeDtypeStruct((b * s, LANES), F32),
                   jax.ShapeDtypeStruct((8, LANES), F32)],
        scratch_shapes=[pltpu.VMEM((8, LANES), F32)],
        compiler_params=_cparams(("arbitrary", "arbitrary")),
        name="moe_router",
    )(x, sc, sh, w_split)


def _row_copy(src_ref, src_row, dst_ref, dst_row, sem):
    return pltpu.make_async_copy(src_ref.at[pl.ds(src_row, 1)], dst_ref.at[pl.ds(dst_row, 1)], sem)


def _dispatch_kernel(dest_ref, pend_ref, h_ref, xs_ref, zero_ref, sem, zero_sem):
    tm = h_ref.shape[0]

    @pl.when(pl.program_id(0) == 0)
    def _():
        zero_ref[...] = jnp.zeros(zero_ref.shape, F32)

        def fill(e):
            end = pend_ref[e]
            nonempty = end > (pend_ref[e - 1] if e else 0)
            start = pl.multiple_of(jnp.maximum(end - MOE_ROWS, 0), MOE_ROWS)
            return nonempty, pltpu.make_async_copy(zero_ref, xs_ref.at[pl.ds(start, MOE_ROWS)],
                                                   zero_sem)

        for e in range(N_EXPERTS):
            nonempty, copy = fill(e)
            pl.when(nonempty)(copy.start)
        for e in range(N_EXPERTS):
            nonempty, copy = fill(e)
            pl.when(nonempty)(copy.wait)

    def issue(i, carry):
        for k in range(TOP_K):
            _row_copy(h_ref, i, xs_ref, dest_ref[TOP_K * i + k], sem).start(priority=k)
        return carry

    def drain(i, carry):
        for k in range(TOP_K):
            _row_copy(h_ref, 0, xs_ref, 0, sem).wait()
        return carry

    lax.fori_loop(0, tm, issue, 0, unroll=8)
    lax.fori_loop(0, tm, drain, 0, unroll=8)


def _dispatch(h, dest, pends, n_rows, *, tm=1024):
    t, d = h.shape
    tm = min(tm, t)
    return pl.pallas_call(
        _dispatch_kernel,
        grid=(t // tm,),
        in_specs=[pl.BlockSpec((TOP_K * tm,), lambda i: (i,), memory_space=pltpu.SMEM),
                  pl.BlockSpec(memory_space=pltpu.SMEM),
                  pl.BlockSpec((tm, d), lambda i: (i, 0))],
        out_specs=pl.BlockSpec(memory_space=pl.ANY),
        out_shape=jax.ShapeDtypeStruct((n_rows, d), F32),
        scratch_shapes=[pltpu.VMEM((MOE_ROWS, d), F32), pltpu.SemaphoreType.DMA,
                        pltpu.SemaphoreType.DMA],
        compiler_params=_cparams(("arbitrary",)),
        name="moe_dispatch",
    )(dest, pends, h)


def _expert_kernel(be_ref, bv_ref, xs_ref, wg_ref, wu_ref, wd_ref, y_ref, *, f_chunk):
    del be_ref
    i = pl.program_id(0)

    @pl.when(bv_ref[i] != 0)
    def _():
        x = xs_ref[...].astype(BF16)
        f = wg_ref.shape[2]
        acc = jnp.zeros(y_ref.shape, F32)
        for c0 in range(0, f, f_chunk):
            gate = jnp.dot(x, wg_ref[0, :, c0:c0 + f_chunk], preferred_element_type=F32)
            up = jnp.dot(x, wu_ref[0, :, c0:c0 + f_chunk], preferred_element_type=F32)
            acc = acc + jnp.dot((_silu(gate) * up).astype(BF16), wd_ref[0, c0:c0 + f_chunk, :],
                                preferred_element_type=F32)
        y_ref[...] = acc

    @pl.when(bv_ref[i] == 0)
    def _():
        y_ref[...] = jnp.zeros_like(y_ref)


def _experts(xs, block_exp, block_valid, wg, wu, wd, *, f_chunk=512):
    n_rows, d = xs.shape
    f = wg.shape[2]
    n_blocks = n_rows // MOE_ROWS
    wspec = lambda shape: pl.BlockSpec(shape, lambda i, be, bv: (be[i], 0, 0),
                                       pipeline_mode=pl.Buffered(1))
    return pl.pallas_call(
        functools.partial(_expert_kernel, f_chunk=f_chunk),
        grid_spec=pltpu.PrefetchScalarGridSpec(
            num_scalar_prefetch=2,
            grid=(n_blocks,),
            in_specs=[pl.BlockSpec((MOE_ROWS, d), lambda i, be, bv: (i, 0)),
                      wspec((1, d, f)), wspec((1, d, f)), wspec((1, f, d))],
            out_specs=pl.BlockSpec((MOE_ROWS, d), lambda i, be, bv: (i, 0))),
        out_shape=jax.ShapeDtypeStruct((n_rows, d), F32),
        compiler_params=_cparams(("arbitrary",)),
        name="moe_experts",
    )(block_exp, block_valid, xs, wg, wu, wd)


def _combine_kernel(dest_ref, next_dest_ref, info_ref, x_ref, g_ref, fg_ref, y_ref, o_ref, buf_ref,
                    sem):
    tm = x_ref.shape[0]
    step = pl.program_id(0)
    slot = step % 2

    def gather(idx_ref, to_slot):
        def issue(i, carry):
            for k in range(TOP_K):
                _row_copy(y_ref, idx_ref[TOP_K * i + k], buf_ref.at[to_slot, k], i,
                          sem.at[to_slot]).start(priority=k)
            return carry
        lax.fori_loop(0, tm, issue, 0, unroll=8)

    @pl.when(step == 0)
    def _():
        gather(dest_ref, 0)

    @pl.when(step + 1 < pl.num_programs(0))
    def _():
        gather(next_dest_ref, 1 - slot)

    def drain(i, carry):
        for k in range(TOP_K):
            _row_copy(y_ref, 0, buf_ref.at[slot, k], 0, sem.at[slot]).wait()
        return carry

    lax.fori_loop(0, tm, drain, 0, unroll=8)
    info = info_ref[...]
    ffn = info[:, 2:3] * buf_ref[slot, 0] + info[:, 3:4] * buf_ref[slot, 1]
    x = x_ref[...] + g_ref[0] * ffn
    o_ref[...] = x * lax.rsqrt(jnp.mean(x * x, axis=-1, keepdims=True) + EPS) * fg_ref[...]


def _combine(dest, info, x2d, g, final_g, y, seq, *, tm=512):
    t, d = x2d.shape
    tm = min(tm, seq)
    per_seq = seq // tm
    n_tiles = t // tm
    return pl.pallas_call(
        _combine_kernel,
        grid=(n_tiles,),
        in_specs=[pl.BlockSpec((TOP_K * tm,), lambda i: (i,), memory_space=pltpu.SMEM),
                  pl.BlockSpec((TOP_K * tm,), lambda i: (jnp.minimum(i + 1, n_tiles - 1),),
                               memory_space=pltpu.SMEM),
                  pl.BlockSpec((tm, LANES), lambda i: (i, 0)),
                  pl.BlockSpec((tm, d), lambda i: (i, 0)),
                  pl.BlockSpec((1, 1, d), lambda i: (i // per_seq, 0, 0)),
                  pl.BlockSpec((1, d), lambda i: (0, 0)),
                  pl.BlockSpec(memory_space=pl.ANY)],
        out_specs=pl.BlockSpec((tm, d), lambda i: (i, 0)),
        out_shape=jax.ShapeDtypeStruct((t, d), F32),
        scratch_shapes=[pltpu.VMEM((2, TOP_K, tm, d), F32), pltpu.SemaphoreType.DMA((2,))],
        compiler_params=_cparams(("arbitrary",)),
        name="moe_combine",
    )(dest, dest, info, x2d, g, final_g[None, :], y)


def _moe_plan(info, counts):
    e = info[:, 0:TOP_K].astype(jnp.int32)
    rank = info[:, 4:4 + TOP_K].astype(jnp.int32)
    cnt = counts[0, :N_EXPERTS].astype(jnp.int32)
    padded = (cnt + MOE_ROWS - 1) // MOE_ROWS * MOE_ROWS
    pends = jnp.cumsum(padded)
    pstarts = pends - padded
    start_of = jnp.zeros_like(e)
    for j in range(N_EXPERTS):
        start_of = jnp.where(e == j, pstarts[j], start_of)
    dest = (start_of + rank).reshape(-1)
    n_blocks = (TOP_K * info.shape[0]) // MOE_ROWS + N_EXPERTS
    block_start = jnp.arange(n_blocks, dtype=jnp.int32) * MOE_ROWS
    valid = block_start < pends[-1]
    last_start = jnp.maximum(pends[-1] - MOE_ROWS, 0)
    start = jnp.where(valid, block_start, last_start)
    block_exp = jnp.sum((start[:, None] >= pends[None, :]).astype(jnp.int32), axis=1)
    block_exp = jnp.minimum(block_exp, N_EXPERTS - 1)
    return dest, pends, block_exp, valid.astype(jnp.int32), n_blocks * MOE_ROWS


def kernel(x, c, ada_w0, ada_b0, m_w_in, m_conv_w, m_conv_b, m_dt_bias, m_a_log, m_d_skip, m_norm_g, m_w_out, ffn_w_gate, ffn_w_up, ffn_w_down, ada_w1, ada_b1, a_w_qkv, a_lam_q1, a_lam_k1, a_lam_q2, a_lam_k2, a_subln_g, a_w_o, moe_w_router, moe_w_gate, moe_w_up, moe_w_down, final_g):
    b, s, d = x.shape
    bf = lambda w: w.astype(BF16)
    c8 = jnp.pad(c, ((0, 8 - b), (0, 0)))

    def mods(w, bias):
        mod = _adaln(c8, w, bias[None, :])[:b]
        return [m[:, None, :] for m in jnp.split(mod, 6, axis=-1)]

    sh1, sc1, g1, sh2, sc2, g2 = mods(ada_w0, ada_b0)
    w_dt = jnp.pad(m_w_in[:, M_MAIN_W:], ((0, 0), (0, LANES - M_HEADS)))
    main, dt_raw = _norm_proj(x, sc1, sh1, bf(m_w_in[:, :M_MAIN_W]), bf(w_dt))
    y = _ssd(main, dt_raw, m_conv_w, m_conv_b, m_dt_bias, m_a_log, m_d_skip, m_norm_g)
    x = _proj_res(y, bf(m_w_out), x, g1)
    x = _swiglu(x, sc2, sh2, g2, bf(ffn_w_gate), bf(ffn_w_up), bf(ffn_w_down))

    sh1, sc1, g1, sh2, sc2, g2 = mods(ada_w1, ada_b1)
    lambda_init = 0.8 - 0.6 * math.exp(-0.3 * 1)
    qkv = _norm_proj(x, sc1, sh1, bf(a_w_qkv), head_major=True)
    lam_pack = jnp.pad(jnp.stack([a_lam_q1, a_lam_k1, a_lam_q2, a_lam_k2]),
                       ((0, 4), (0, LANES - A_HEAD_DIM)))
    att = _attention(qkv, lam_pack, a_subln_g, lambda_init)
    x = _proj_res(att, bf(a_w_o), x, g1)

    h, info, counts = _router(x, sc2, sh2, moe_w_router)
    dest, pends, block_exp, block_valid, n_rows = _moe_plan(info, counts)
    xs = _dispatch(h, dest, pends, n_rows)
    ye = _experts(xs, block_exp, block_valid, bf(moe_w_gate), bf(moe_w_up), bf(moe_w_down))
    out = _combine(dest, info, x.reshape(b * s, d), g2, final_g, ye, s)
    return out.reshape(b, s, d)
```
